```python
import math
import jax, jax.numpy as jnp
from jax import lax
import numpy as np

D_MODEL = 1024
BATCH = 32
SEQ = 2048
DEPTH = 1

N_HEADS = 8
HEAD_DIM = 64
QK_DIM = 2 * N_HEADS * HEAD_DIM
V_DIM = N_HEADS * 2 * HEAD_DIM
BLOCK_Q = 128
CONV_WIDTH = D_MODEL
CONV_K = 3
D_FF = -(-8 * D_MODEL // (3 * 256)) * 256
EPS = 1e-6

IN_SIZES = [QK_DIM, QK_DIM, V_DIM, CONV_WIDTH, CONV_WIDTH, CONV_WIDTH, D_MODEL, D_MODEL]
IN_SPLITS = [int(s) for s in np.cumsum(IN_SIZES)[:-1]]
D_IN = int(sum(IN_SIZES))

kernel_name = "hybrid_conv_diffattn_gated_adaln_block"


def rmsnorm(x, g):
    xf = x.astype(jnp.float32)
    y = xf * lax.rsqrt(jnp.mean(xf * xf, axis=-1, keepdims=True) + EPS)
    return (y * g.astype(jnp.float32)).astype(x.dtype)


def causal_dwconv(u, w):
    S = u.shape[1]
    up = jnp.pad(u, ((0, 0), (CONV_K - 1, 0), (0, 0)))
    y = up[:, 0:S] * w[0]
    for j in range(1, CONV_K):
        y = y + up[:, j:j + S] * w[j]
    return y


def diff_attention(q, k, v, lam):
    S = q.shape[1]
    scale = HEAD_DIM ** -0.5
    outs = []
    for i in range(S // BLOCK_Q):
        q0 = i * BLOCK_Q
        kl = q0 + BLOCK_Q
        qb = q[:, q0:kl]
        kb = k[:, :kl]
        vb = v[:, :kl]
        s = jnp.einsum('bqhcd,bkhcd->bhcqk', qb, kb).astype(jnp.float32) * scale
        causal = (q0 + jnp.arange(BLOCK_Q))[:, None] >= jnp.arange(kl)[None, :]
        p = jax.nn.softmax(jnp.where(causal, s, -jnp.inf), axis=-1)
        a = (p[:, :, 0] - lam * p[:, :, 1]).astype(vb.dtype)
        outs.append(jnp.einsum('bhqk,bkhe->bqhe', a, vb))
    return jnp.concatenate(outs, axis=1)


def setup_inputs(seed: int = 0) -> dict:
    key = jax.random.key(seed)
    ks = jax.random.split(key, 20)
    f32 = jnp.float32
    L, D = DEPTH, D_MODEL

    def w(k, shape, fan_in, mult=1.0):
        return jax.random.normal(k, shape, f32) * (mult * fan_in ** -0.5)

    def gain(k, shape):
        return 1.0 + 0.02 * jax.random.normal(k, shape, f32)

    return {
        "x": jax.random.normal(ks[0], (BATCH, SEQ, D), f32),
        "c": jax.random.normal(ks[1], (BATCH, D), f32),
        "w_ada": w(ks[2], (L, D, 6 * D), D, 0.1),
        "b_ada": 0.02 * jax.random.normal(ks[3], (L, 6 * D), f32),
        "norm1_g": gain(ks[4], (L, D)),
        "w_in": w(ks[5], (L, D, D_IN), D),
        "conv_w": w(ks[6], (L, CONV_K, CONV_WIDTH), CONV_K),
        "q_norm_g": gain(ks[7], (L, HEAD_DIM)),
        "k_norm_g": gain(ks[8], (L, HEAD_DIM)),
        "lambda_q1": 0.1 * jax.random.normal(ks[9], (L, HEAD_DIM), f32),
        "lambda_k1": 0.1 * jax.random.normal(ks[10], (L, HEAD_DIM), f32),
        "lambda_q2": 0.1 * jax.random.normal(ks[11], (L, HEAD_DIM), f32),
        "lambda_k2": 0.1 * jax.random.normal(ks[12], (L, HEAD_DIM), f32),
        "subln_g": gain(ks[13], (L, 2 * HEAD_DIM)),
        "w_a_out": w(ks[14], (L, CONV_WIDTH, D), CONV_WIDTH),
        "w_b_out": w(ks[15], (L, V_DIM, D), V_DIM),
        "w_o": w(ks[16], (L, D, D), D),
        "norm2_g": gain(ks[17], (L, D)),
        "w_gu": w(ks[18], (L, D, 2 * D_FF), D),
        "w_down": w(ks[19], (L, D_FF, D), D_FF),
    }


def reference(x, c, w_ada, b_ada, norm1_g, w_in, conv_w, q_norm_g, k_norm_g,
              lambda_q1, lambda_k1, lambda_q2, lambda_k2, subln_g,
              w_a_out, w_b_out, w_o, norm2_g, w_gu, w_down):
    B, S, D = x.shape
    c_act = jax.nn.silu(c)
    for l in range(DEPTH):
        lambda_init = 0.8 - 0.6 * math.exp(-0.3 * l)
        mod = c_act @ w_ada[l] + b_ada[l]
        sh1, sc1, g1, sh2, sc2, g2 = [m[:, None, :] for m in jnp.split(mod, 6, axis=-1)]

        h = rmsnorm(x, norm1_g[l]) * (1.0 + sc1) + sh1
        proj = h @ w_in[l]
        q, k, v, cb, cc, cx, ga, gb = jnp.split(proj, IN_SPLITS, axis=-1)

        ya = cb * causal_dwconv(cc * cx, conv_w[l])
        ya = ya @ w_a_out[l]

        q = rmsnorm(q.reshape(B, S, N_HEADS, 2, HEAD_DIM), q_norm_g[l])
        k = rmsnorm(k.reshape(B, S, N_HEADS, 2, HEAD_DIM), k_norm_g[l])
        v = v.reshape(B, S, N_HEADS, 2 * HEAD_DIM)
        lam = (jnp.exp(jnp.sum(lambda_q1[l].astype(jnp.float32) * lambda_k1[l].astype(jnp.float32)))
               - jnp.exp(jnp.sum(lambda_q2[l].astype(jnp.float32) * lambda_k2[l].astype(jnp.float32)))
               + lambda_init)
        o = diff_attention(q, k, v, lam)
        o = rmsnorm(o, subln_g[l]) * (1.0 - lambda_init)
        yb = o.reshape(B, S, V_DIM) @ w_b_out[l]

        m = jax.nn.sigmoid(ga) * ya + jax.nn.sigmoid(gb) * yb
        x = x + g1 * (m @ w_o[l])

        h2 = rmsnorm(x, norm2_g[l]) * (1.0 + sc2) + sh2
        gu = h2 @ w_gu[l]
        fg, fu = jnp.split(gu, 2, axis=-1)
        x = x + g2 * ((jax.nn.silu(fg) * fu) @ w_down[l])
    return x
```

```python
import functools
import math

import jax
import jax.numpy as jnp
from jax import lax
from jax.experimental import pallas as pl
from jax.experimental.pallas import tpu as pltpu

EPS = 1e-6
N_HEADS = 8
HEAD_DIM = 64
CONV_K = 3
LAMBDA_INIT = 0.8 - 0.6 * math.exp(-0.3 * 0)

VMEM_LIMIT_BYTES = 56 * 1024 * 1024
NORM_GROUP_TILE = 256

BF16 = jnp.bfloat16
F32 = jnp.float32


def _dot(a, b):
    return jnp.dot(a, b, preferred_element_type=F32)


def _resident(shape):
    return pl.BlockSpec(shape, lambda *_: (0,) * len(shape), pipeline_mode=pl.Buffered(1))


def _ada_kernel(c_ref, w_ref, b_ref, lq1_ref, lk1_ref, lq2_ref, lk2_ref, mod_ref, lam_ref):
    c = c_ref[...]
    c_act = c * jax.nn.sigmoid(c)
    mod_ref[...] = _dot(c_act.astype(BF16), w_ref[...].astype(BF16)) + b_ref[...]
    s1 = jnp.sum(lq1_ref[...] * lk1_ref[...], axis=-1, keepdims=True)
    s2 = jnp.sum(lq2_ref[...] * lk2_ref[...], axis=-1, keepdims=True)
    lam_ref[...] = jnp.exp(s1) - jnp.exp(s2) + LAMBDA_INIT


def _ada_call(c, w_ada, b_ada, lq1, lk1, lq2, lk2):
    B, D = c.shape
    n_mod = w_ada.shape[1] // D
    vec = pl.BlockSpec((1, HEAD_DIM), lambda j: (0, 0))
    return pl.pallas_call(
        _ada_kernel,
        out_shape=(jax.ShapeDtypeStruct((B, n_mod * D), F32), jax.ShapeDtypeStruct((1, 1), F32)),
        grid=(n_mod,),
        in_specs=[pl.BlockSpec((B, D), lambda j: (0, 0)),
                  pl.BlockSpec((D, D), lambda j: (0, j)),
                  pl.BlockSpec((1, D), lambda j: (0, j)),
                  vec, vec, vec, vec],
        out_specs=(pl.BlockSpec((B, D), lambda j: (0, j)),
                   pl.BlockSpec((1, 1), lambda j: (0, 0))),
        compiler_params=pltpu.CompilerParams(dimension_semantics=("arbitrary",)),
        name="ada",
    )(c, w_ada, b_ada, lq1, lk1, lq2, lk2)


def _inproj_kernel(x_ref, mod_ref, g1_ref, w_ref, gq_ref, gk_ref, gmean_ref, cw_ref,
                   q_ref, k_ref, v_ref, u_ref, sa_ref, sb_ref, carry_ref, *, tiles_per_seq):
    i = pl.program_id(0)
    tm, D = x_ref.shape

    x = x_ref[...]
    inv = lax.rsqrt(jnp.mean(x * x, axis=-1, keepdims=True) + EPS)
    h = (x * inv * g1_ref[...]) * (1.0 + mod_ref[0, 1:2, :]) + mod_ref[0, 0:1, :]
    hb = h.astype(BF16)

    def qk_norm(col0, g_ref, out_ref, scale):
        y = _dot(hb, w_ref[:, col0:col0 + D])
        for c0 in range(0, D, NORM_GROUP_TILE):
            yc = y[:, c0:c0 + NORM_GROUP_TILE]
            ms = _dot((yc * yc).astype(BF16), gmean_ref[...])
            out_ref[:, c0:c0 + NORM_GROUP_TILE] = (
                yc * lax.rsqrt(ms + EPS) * (g_ref[...] * scale)).astype(BF16)

    qk_norm(0, gq_ref, q_ref, HEAD_DIM ** -0.5)
    qk_norm(D, gk_ref, k_ref, 1.0)

    v_ref[...] = _dot(hb, w_ref[:, 2 * D:3 * D]).astype(BF16)

    @pl.when(i % tiles_per_seq == 0)
    def _():
        carry_ref[...] = jnp.zeros_like(carry_ref)

    prev = carry_ref[...]
    cb = _dot(hb, w_ref[:, 3 * D:4 * D])
    p = _dot(hb, w_ref[:, 4 * D:5 * D]) * _dot(hb, w_ref[:, 5 * D:6 * D])
    carry_ref[...] = p[tm - 8:, :]
    row = lax.broadcasted_iota(jnp.int32, (tm, D), 0)
    p1 = jnp.where(row == 0, prev[7:8, :], pltpu.roll(p, 1, axis=0))
    p2 = jnp.where(row == 0, prev[6:7, :], jnp.where(row == 1, prev[7:8, :], pltpu.roll(p, 2, axis=0)))
    conv = cw_ref[0:1, :] * p2 + cw_ref[1:2, :] * p1 + cw_ref[2:3, :] * p
    u_ref[...] = (cb * conv).astype(BF16)

    sa_ref[...] = jax.nn.sigmoid(_dot(hb, w_ref[:, 6 * D:7 * D])).astype(BF16)
    sb_ref[...] = jax.nn.sigmoid(_dot(hb, w_ref[:, 7 * D:8 * D])).astype(BF16)


def _inproj_call(x2, mod8, g1, w_in, gq, gk, gmean, conv_w, *, seq, tm):
    T, D = x2.shape
    tiles_per_seq = seq // tm
    row_tile = pl.BlockSpec((tm, D), lambda i: (i, 0))
    out = jax.ShapeDtypeStruct((T, D), BF16)
    return pl.pallas_call(
        functools.partial(_inproj_kernel, tiles_per_seq=tiles_per_seq),
        out_shape=(out,) * 6,
        grid=(T // tm,),
        in_specs=[row_tile,
                  pl.BlockSpec((1, 8, D), lambda i: (i // tiles_per_seq, 0, 0)),
                  _resident((1, D)),
                  _resident(w_in.shape),
                  _resident((1, NORM_GROUP_TILE)),
                  _resident((1, NORM_GROUP_TILE)),
                  _resident((NORM_GROUP_TILE, NORM_GROUP_TILE)),
                  _resident(conv_w.shape)],
        out_specs=(row_tile,) * 6,
        scratch_shapes=[pltpu.VMEM((8, D), F32)],
        compiler_params=pltpu.CompilerParams(dimension_semantics=("arbitrary",),
                                             vmem_limit_bytes=VMEM_LIMIT_BYTES),
        name="inproj",
    )(x2, mod8, g1, w_in, gq, gk, gmean, conv_w)


def _attn_kernel(lam_ref, q_ref, k_ref, v_ref, g_ref, o_ref, *, tq):
    qi = pl.program_id(2)
    d2 = 2 * HEAD_DIM
    q = q_ref[0]
    lane = lax.broadcasted_iota(jnp.int32, q.shape, 1)
    zero = jnp.zeros_like(q)
    qq = jnp.concatenate([jnp.where(lane < HEAD_DIM, q, zero),
                          jnp.where(lane >= HEAD_DIM, q, zero)], axis=0)

    def step(j, carry, masked):
        m, l, acc = carry
        start = pl.multiple_of(j * tq, tq)
        k = k_ref[0, pl.ds(start, tq), :]
        v = v_ref[0, pl.ds(start, tq), :]
        s = lax.dot_general(k, qq, (((1,), (1,)), ((), ())), preferred_element_type=F32)
        if masked:
            kpos = lax.broadcasted_iota(jnp.int32, s.shape, 0)
            qpos = lax.broadcasted_iota(jnp.int32, s.shape, 1) % tq
            s = jnp.where(qpos >= kpos, s, -jnp.inf)
        m_new = jnp.maximum(m, jnp.max(s, axis=0, keepdims=True))
        alpha = jnp.exp(m - m_new)
        p = jnp.exp(s - m_new)
        l = alpha * l + jnp.sum(p, axis=0, keepdims=True)
        pv = lax.dot_general(v, p.astype(BF16), (((0,), (0,)), ((), ())), preferred_element_type=F32)
        return m_new, l, alpha * acc + pv

    init = (jnp.full((1, 2 * tq), -jnp.inf, F32), jnp.zeros((1, 2 * tq), F32),
            jnp.zeros((d2, 2 * tq), F32))
    carry = lax.fori_loop(0, qi, lambda j, c: step(j, c, False), init)
    _, l, acc = step(qi, carry, True)

    o = acc / l
    od = o[:, :tq] - lam_ref[0, 0] * o[:, tq:]
    inv = lax.rsqrt(jnp.mean(od * od, axis=0, keepdims=True) + EPS)
    y = od * inv * (g_ref[...] * (1.0 - LAMBDA_INIT))
    o_ref[0] = y.T.astype(BF16)


def _attn_call(lam, q, k, v, subln_col, *, tq):
    B, S, _ = q.shape
    d2 = 2 * HEAD_DIM
    seq_block = pl.BlockSpec((1, S, d2), lambda b, h, i: (b, 0, h))
    q_block = pl.BlockSpec((1, tq, d2), lambda b, h, i: (b, i, h))
    return pl.pallas_call(
        functools.partial(_attn_kernel, tq=tq),
        out_shape=jax.ShapeDtypeStruct(q.shape, BF16),
        grid=(B, N_HEADS, S // tq),
        in_specs=[pl.BlockSpec(memory_space=pltpu.SMEM),
                  q_block, seq_block, seq_block,
                  pl.BlockSpec((d2, 1), lambda b, h, i: (0, 0))],
        out_specs=q_block,
        compiler_params=pltpu.CompilerParams(
            dimension_semantics=("arbitrary", "arbitrary", "arbitrary"),
            vmem_limit_bytes=VMEM_LIMIT_BYTES),
        name="attn",
    )(lam, q, k, v, subln_col)


def _outffn_kernel(x_ref, u_ref, o_ref, sa_ref, sb_ref, mod_ref, g2_ref,
                   wa_ref, wb_ref, wo_ref, wgu_ref, wd_ref, out_ref, *, ff_chunks):
    d_ff = wd_ref.shape[0]
    ya = _dot(u_ref[...], wa_ref[...])
    yb = _dot(o_ref[...], wb_ref[...])
    m = sa_ref[...].astype(F32) * ya + sb_ref[...].astype(F32) * yb
    x1 = x_ref[...] + mod_ref[0, 2:3, :] * _dot(m.astype(BF16), wo_ref[...])

    inv = lax.rsqrt(jnp.mean(x1 * x1, axis=-1, keepdims=True) + EPS)
    h2 = ((x1 * inv * g2_ref[...]) * (1.0 + mod_ref[0, 4:5, :]) + mod_ref[0, 3:4, :]).astype(BF16)

    acc = None
    for c0, c1 in ff_chunks:
        fg = _dot(h2, wgu_ref[:, c0:c1])
        fu = _dot(h2, wgu_ref[:, d_ff + c0:d_ff + c1])
        a = (fg * jax.nn.sigmoid(fg) * fu).astype(BF16)
        part = _dot(a, wd_ref[c0:c1, :])
        acc = part if acc is None else acc + part
    out_ref[...] = x1 + mod_ref[0, 5:6, :] * acc


def _outffn_call(x2, u, o, sa, sb, mod8, g2, wa, wb, wo, wgu, wd, *, seq, tm, ff_chunks):
    T, D = x2.shape
    tiles_per_seq = seq // tm
    row_tile = pl.BlockSpec((tm, D), lambda i: (i, 0))
    return pl.pallas_call(
        functools.partial(_outffn_kernel, ff_chunks=ff_chunks),
        out_shape=jax.ShapeDtypeStruct((T, D), F32),
        grid=(T // tm,),
        in_specs=[row_tile, row_tile, row_tile, row_tile, row_tile,
                  pl.BlockSpec((1, 8, D), lambda i: (i // tiles_per_seq, 0, 0)),
                  _resident((1, D)),
                  _resident(wa.shape), _resident(wb.shape), _resident(wo.shape),
                  _resident(wgu.shape), _resident(wd.shape)],
        out_specs=row_tile,
        compiler_params=pltpu.CompilerParams(dimension_semantics=("arbitrary",),
                                             vmem_limit_bytes=VMEM_LIMIT_BYTES),
        name="outffn",
    )(x2, u, o, sa, sb, mod8, g2, wa, wb, wo, wgu, wd)


def _ff_chunks(d_ff, max_chunk):
    n = -(-d_ff // max_chunk)
    tiles = d_ff // NORM_GROUP_TILE
    assert tiles * NORM_GROUP_TILE == d_ff
    bounds = [NORM_GROUP_TILE * (tiles * c // n) for c in range(n + 1)]
    return tuple(zip(bounds[:-1], bounds[1:]))


def kernel(x, c, w_ada, b_ada, norm1_g, w_in, conv_w, q_norm_g, k_norm_g, lambda_q1, lambda_k1,
           lambda_q2, lambda_k2, subln_g, w_a_out, w_b_out, w_o, norm2_g, w_gu, w_down):
    B, S, D = x.shape
    assert w_ada.shape[0] == 1, "single-layer block"
    d_ff = w_down.shape[1]
    tm = 512
    tq = 256

    mod, lam = _ada_call(c, w_ada[0], b_ada[0][None, :], lambda_q1, lambda_k1, lambda_q2, lambda_k2)
    mod8 = jnp.pad(mod.reshape(B, 6, D), ((0, 0), (0, 2), (0, 0)))

    reps = NORM_GROUP_TILE // HEAD_DIM
    gq = jnp.tile(q_norm_g[0], reps)[None, :]
    gk = jnp.tile(k_norm_g[0], reps)[None, :]
    grp = jnp.arange(NORM_GROUP_TILE) // HEAD_DIM
    gmean = jnp.where(grp[:, None] == grp[None, :], 1.0 / HEAD_DIM, 0.0).astype(BF16)

    x2 = x.reshape(B * S, D)
    q, k, v, u, sa, sb = _inproj_call(x2, mod8, norm1_g, w_in[0].astype(BF16), gq, gk, gmean,
                                      conv_w[0], seq=S, tm=tm)

    o = _attn_call(lam, q.reshape(B, S, D), k.reshape(B, S, D), v.reshape(B, S, D),
                   subln_g[0][:, None], tq=tq)

    out = _outffn_call(x2, u, o.reshape(B * S, D), sa, sb, mod8, norm2_g,
                       w_a_out[0].astype(BF16), w_b_out[0].astype(BF16), w_o[0].astype(BF16),
                       w_gu[0].astype(BF16), w_down[0].astype(BF16),
                       seq=S, tm=tm, ff_chunks=_ff_chunks(d_ff, 1536))
    return out.reshape(B, S, D)
```

```python
import functools
import math

import jax
import jax.numpy as jnp
from jax import lax
from jax.experimental import pallas as pl
from jax.experimental.pallas import tpu as pltpu

EPS = 1e-6
N_HEADS = 8
HEAD_DIM = 64
CONV_K = 3
LAMBDA_INIT = 0.8 - 0.6 * math.exp(-0.3 * 0)
LOG2_E = math.log2(math.e)

VMEM_LIMIT_BYTES = 56 * 1024 * 1024
NORM_GROUP_TILE = 256

BF16 = jnp.bfloat16
F32 = jnp.float32


def _dot(a, b):
    return jnp.dot(a, b, preferred_element_type=F32)


def _resident(shape):
    return pl.BlockSpec(shape, lambda *_: (0,) * len(shape), pipeline_mode=pl.Buffered(1))


def _ada_kernel(c_ref, w_ref, b_ref, lq1_ref, lk1_ref, lq2_ref, lk2_ref, mod_ref, lam_ref):
    c = c_ref[...]
    c_act = c * jax.nn.sigmoid(c)
    mod_ref[...] = _dot(c_act.astype(BF16), w_ref[...].astype(BF16)) + b_ref[...]
    s1 = jnp.sum(lq1_ref[...] * lk1_ref[...], axis=-1, keepdims=True)
    s2 = jnp.sum(lq2_ref[...] * lk2_ref[...], axis=-1, keepdims=True)
    lam_ref[...] = jnp.exp(s1) - jnp.exp(s2) + LAMBDA_INIT


def _ada_call(c, w_ada, b_ada, lq1, lk1, lq2, lk2):
    B, D = c.shape
    n_mod = w_ada.shape[1] // D
    vec = pl.BlockSpec((1, HEAD_DIM), lambda j: (0, 0))
    return pl.pallas_call(
        _ada_kernel,
        out_shape=(jax.ShapeDtypeStruct((B, n_mod * D), F32), jax.ShapeDtypeStruct((1, 1), F32)),
        grid=(n_mod,),
        in_specs=[pl.BlockSpec((B, D), lambda j: (0, 0)),
                  pl.BlockSpec((D, D), lambda j: (0, j)),
                  pl.BlockSpec((1, D), lambda j: (0, j)),
                  vec, vec, vec, vec],
        out_specs=(pl.BlockSpec((B, D), lambda j: (0, j)),
                   pl.BlockSpec((1, 1), lambda j: (0, 0))),
        compiler_params=pltpu.CompilerParams(dimension_semantics=("arbitrary",)),
        name="ada",
    )(c, w_ada, b_ada, lq1, lk1, lq2, lk2)


def _inproj_kernel(x_ref, mod_ref, g1_ref, w_ref, gq_ref, gk_ref, gmean_ref, cw_ref,
                   q_ref, k_ref, v_ref, u_ref, sa_ref, sb_ref, carry_ref, *, tiles_per_seq):
    i = pl.program_id(0)
    tm, D = x_ref.shape

    x = x_ref[...]
    inv = lax.rsqrt(jnp.mean(x * x, axis=-1, keepdims=True) + EPS)
    h = (x * inv * g1_ref[...]) * (1.0 + mod_ref[0, 1:2, :]) + mod_ref[0, 0:1, :]
    hb = h.astype(BF16)

    def qk_norm(col0, g_ref, out_ref, scale):
        y = _dot(hb, w_ref[:, col0:col0 + D])
        for c0 in range(0, D, NORM_GROUP_TILE):
            yc = y[:, c0:c0 + NORM_GROUP_TILE]
            ms = _dot((yc * yc).astype(BF16), gmean_ref[...])
            out_ref[:, c0:c0 + NORM_GROUP_TILE] = (
                yc * lax.rsqrt(ms + EPS) * (g_ref[...] * scale)).astype(BF16)

    qk_norm(0, gq_ref, q_ref, HEAD_DIM ** -0.5 * LOG2_E)
    qk_norm(D, gk_ref, k_ref, 1.0)

    v_ref[...] = _dot(hb, w_ref[:, 2 * D:3 * D]).astype(BF16)

    @pl.when(i % tiles_per_seq == 0)
    def _():
        carry_ref[...] = jnp.zeros_like(carry_ref)

    prev = carry_ref[...]
    cb = _dot(hb, w_ref[:, 3 * D:4 * D])
    p = _dot(hb, w_ref[:, 4 * D:5 * D]) * _dot(hb, w_ref[:, 5 * D:6 * D])
    carry_ref[...] = p[tm - 8:, :]
    row = lax.broadcasted_iota(jnp.int32, (tm, D), 0)
    p1 = jnp.where(row == 0, prev[7:8, :], pltpu.roll(p, 1, axis=0))
    p2 = jnp.where(row == 0, prev[6:7, :], jnp.where(row == 1, prev[7:8, :], pltpu.roll(p, 2, axis=0)))
    conv = cw_ref[0:1, :] * p2 + cw_ref[1:2, :] * p1 + cw_ref[2:3, :] * p
    u_ref[...] = (cb * conv).astype(BF16)

    sa_ref[...] = jax.nn.sigmoid(_dot(hb, w_ref[:, 6 * D:7 * D])).astype(BF16)
    sb_ref[...] = jax.nn.sigmoid(_dot(hb, w_ref[:, 7 * D:8 * D])).astype(BF16)


def _inproj_call(x2, mod8, g1, w_in, gq, gk, gmean, conv_w, *, seq, tm):
    T, D = x2.shape
    tiles_per_seq = seq // tm
    row_tile = pl.BlockSpec((tm, D), lambda i: (i, 0))
    out = jax.ShapeDtypeStruct((T, D), BF16)
    return pl.pallas_call(
        functools.partial(_inproj_kernel, tiles_per_seq=tiles_per_seq),
        out_shape=(out,) * 6,
        grid=(T // tm,),
        in_specs=[row_tile,
                  pl.BlockSpec((1, 8, D), lambda i: (i // tiles_per_seq, 0, 0)),
                  _resident((1, D)),
                  _resident(w_in.shape),
                  _resident((1, NORM_GROUP_TILE)),
                  _resident((1, NORM_GROUP_TILE)),
                  _resident((NORM_GROUP_TILE, NORM_GROUP_TILE)),
                  _resident(conv_w.shape)],
        out_specs=(row_tile,) * 6,
        scratch_shapes=[pltpu.VMEM((8, D), F32)],
        compiler_params=pltpu.CompilerParams(dimension_semantics=("arbitrary",),
                                             vmem_limit_bytes=VMEM_LIMIT_BYTES),
        name="inproj",
    )(x2, mod8, g1, w_in, gq, gk, gmean, conv_w)


def _attn_kernel(lam_ref, q_ref, k_ref, v_ref, g_ref, o_ref, acc_ref, *, tq, heads):
    qi = pl.program_id(2)
    d2 = 2 * HEAD_DIM
    lane = lax.broadcasted_iota(jnp.int32, (tq, d2), 1)
    qqs = []
    for hh in range(heads):
        q = q_ref[0, :, hh * d2:(hh + 1) * d2]
        zero = jnp.zeros_like(q)
        qqs.append(jnp.concatenate([jnp.where(lane < HEAD_DIM, q, zero),
                                    jnp.where(lane >= HEAD_DIM, q, zero)], axis=0))

    def head_update(hh, start, s, stats, diagonal):
        if diagonal:
            kpos = lax.broadcasted_iota(jnp.int32, s.shape, 0)
            qpos = lax.broadcasted_iota(jnp.int32, s.shape, 1) % tq
            s = jnp.where(qpos >= kpos, s, -jnp.inf)
            m_new = jnp.max(s, axis=0, keepdims=True)
        else:
            m, l = stats
            m_new = jnp.maximum(m, jnp.max(s, axis=0, keepdims=True))
            alpha = jnp.exp2(m - m_new)
        p = jnp.exp2(s - m_new)
        p_sum = jnp.sum(p, axis=0, keepdims=True)
        v = v_ref[0, pl.ds(start, tq), hh * d2:(hh + 1) * d2]
        pv = lax.dot_general(v, p.astype(BF16), (((0,), (0,)), ((), ())), preferred_element_type=F32)
        if diagonal:
            acc_ref[hh] = pv
            return m_new, p_sum
        acc_ref[hh] = alpha * acc_ref[hh] + pv
        return m_new, alpha * l + p_sum

    def step(j, stats, diagonal):
        start = pl.multiple_of(j * tq, tq)
        ss =[lax.dot_general(k_ref[0, pl.ds(start, tq), hh * d2:(hh + 1) * d2], qqs[hh],
                              (((1,), (1,)), ((), ())), preferred_element_type=F32)
              for hh in range(heads)]
        return tuple(head_update(hh, start, ss[hh], stats[hh], diagonal) for hh in range(heads))

    zeros = jnp.zeros((1, 2 * tq), F32)
    stats = lax.fori_loop(qi, qi + 1, lambda j, st: step(j, st, True), ((zeros, zeros),) * heads)
    stats = lax.fori_loop(0, qi, lambda j, st: step(j, st, False), stats)

    lam = lam_ref[0, 0]
    for hh in range(heads):
        o = acc_ref[hh] * (1.0 / stats[hh][1])
        od = o[:, :tq] - lam * o[:, tq:]
        inv = lax.rsqrt(jnp.mean(od * od, axis=0, keepdims=True) + EPS)
        y = od * inv * (g_ref[...] * (1.0 - LAMBDA_INIT))
        o_ref[0, :, hh * d2:(hh + 1) * d2] = y.T.astype(BF16)


def _attn_call(lam, q, k, v, subln_col, *, tq, heads):
    B, S, _ = q.shape
    d2 = 2 * HEAD_DIM * heads
    seq_block = pl.BlockSpec((1, S, d2), lambda b, h, i: (b, 0, h))
    q_block = pl.BlockSpec((1, tq, d2), lambda b, h, i: (b, i, h))
    return pl.pallas_call(
        functools.partial(_attn_kernel, tq=tq, heads=heads),
        out_shape=jax.ShapeDtypeStruct(q.shape, BF16),
        grid=(B, N_HEADS // heads, S // tq),
        in_specs=[pl.BlockSpec(memory_space=pltpu.SMEM),
                  q_block, seq_block, seq_block,
                  pl.BlockSpec((2 * HEAD_DIM, 1), lambda b, h, i: (0, 0))],
        out_specs=q_block,
        scratch_shapes=[pltpu.VMEM((heads, 2 * HEAD_DIM, 2 * tq), F32)],
        compiler_params=pltpu.CompilerParams(
            dimension_semantics=("arbitrary", "arbitrary", "arbitrary"),
            vmem_limit_bytes=VMEM_LIMIT_BYTES),
        name="attn",
    )(lam, q, k, v, subln_col)


def _outffn_kernel(x_ref, u_ref, o_ref, sa_ref, sb_ref, mod_ref, g2_ref,
                   wa_ref, wb_ref, wo_ref, wgu_ref, wd_ref, out_ref, *, ff_chunks):
    d_ff = wd_ref.shape[0]
    ya = _dot(u_ref[...], wa_ref[...])
    yb = _dot(o_ref[...], wb_ref[...])
    m = sa_ref[...].astype(F32) * ya + sb_ref[...].astype(F32) * yb
    x1 = x_ref[...] + mod_ref[0, 2:3, :] * _dot(m.astype(BF16), wo_ref[...])

    inv = lax.rsqrt(jnp.mean(x1 * x1, axis=-1, keepdims=True) + EPS)
    h2 = ((x1 * inv * g2_ref[...]) * (1.0 + mod_ref[0, 4:5, :]) + mod_ref[0, 3:4, :]).astype(BF16)

    acc = None
    for c0, c1 in ff_chunks:
        fg = _dot(h2, wgu_ref[:, c0:c1])
        fu = _dot(h2, wgu_ref[:, d_ff + c0:d_ff + c1])
        a = (fg * jax.nn.sigmoid(fg) * fu).astype(BF16)
        part = _dot(a, wd_ref[c0:c1, :])
        acc = part if acc is None else acc + part
    out_ref[...] = x1 + mod_ref[0, 5:6, :] * acc


def _outffn_call(x2, u, o, sa, sb, mod8, g2, wa, wb, wo, wgu, wd, *, seq, tm, ff_chunks):
    T, D = x2.shape
    tiles_per_seq = seq // tm
    row_tile = pl.BlockSpec((tm, D), lambda i: (i, 0))
    return pl.pallas_call(
        functools.partial(_outffn_kernel, ff_chunks=ff_chunks),
        out_shape=jax.ShapeDtypeStruct((T, D), F32),
        grid=(T // tm,),
        in_specs=[row_tile, row_tile, row_tile, row_tile, row_tile,
                  pl.BlockSpec((1, 8, D), lambda i: (i // tiles_per_seq, 0, 0)),
                  _resident((1, D)),
                  _resident(wa.shape), _resident(wb.shape), _resident(wo.shape),
                  _resident(wgu.shape), _resident(wd.shape)],
        out_specs=row_tile,
        compiler_params=pltpu.CompilerParams(dimension_semantics=("arbitrary",),
                                             vmem_limit_bytes=VMEM_LIMIT_BYTES),
        name="outffn",
    )(x2, u, o, sa, sb, mod8, g2, wa, wb, wo, wgu, wd)


def _ff_chunks(d_ff, max_chunk):
    n = -(-d_ff // max_chunk)
    tiles = d_ff // NORM_GROUP_TILE
    assert tiles * NORM_GROUP_TILE == d_ff
    bounds = [NORM_GROUP_TILE * (tiles * c // n) for c in range(n + 1)]
    return tuple(zip(bounds[:-1], bounds[1:]))


def kernel(x, c, w_ada, b_ada, norm1_g, w_in, conv_w, q_norm_g, k_norm_g, lambda_q1, lambda_k1,
           lambda_q2, lambda_k2, subln_g, w_a_out, w_b_out, w_o, norm2_g, w_gu, w_down):
    B, S, D = x.shape
    assert w_ada.shape[0] == 1, "single-layer block"
    d_ff = w_down.shape[1]
    tm = 512
    tq = 256

    mod, lam = _ada_call(c, w_ada[0], b_ada[0][None, :], lambda_q1, lambda_k1, lambda_q2, lambda_k2)
    mod8 = jnp.pad(mod.reshape(B, 6, D), ((0, 0), (0, 2), (0, 0)))

    reps = NORM_GROUP_TILE // HEAD_DIM
    gq = jnp.tile(q_norm_g[0], reps)[None, :]
    gk = jnp.tile(k_norm_g[0], reps)[None, :]
    grp = jnp.arange(NORM_GROUP_TILE) // HEAD_DIM
    gmean = jnp.where(grp[:, None] == grp[None, :], 1.0 / HEAD_DIM, 0.0).astype(BF16)

    x2 = x.reshape(B * S, D)
    q, k, v, u, sa, sb = _inproj_call(x2, mod8, norm1_g, w_in[0].astype(BF16), gq, gk, gmean,
                                      conv_w[0], seq=S, tm=tm)

    o = _attn_call(lam, q.reshape(B, S, D), k.reshape(B, S, D), v.reshape(B, S, D),
                   subln_g[0][:, None], tq=tq, heads=8)

    out = _outffn_call(x2, u, o.reshape(B * S, D), sa, sb, mod8, norm2_g,
                       w_a_out[0].astype(BF16), w_b_out[0].astype(BF16), w_o[0].astype(BF16),
                       w_gu[0].astype(BF16), w_down[0].astype(BF16),
                       seq=S, tm=tm, ff_chunks=_ff_chunks(d_ff, 1536))
    return out.reshape(B, S, D)
```

```python
import functools
import math

import jax
import jax.numpy as jnp
from jax import lax
from jax.experimental import pallas as pl
from jax.experimental.pallas import tpu as pltpu

EPS = 1e-6
N_HEADS = 8
HEAD_DIM = 64
CONV_K = 3
LAMBDA_INIT = 0.8 - 0.6 * math.exp(-0.3 * 0)
LOG2_E = math.log2(math.e)
QK_SCALE = HEAD_DIM ** -0.5 * LOG2_E
MAX_UNSHIFTED_LOGIT = 60.0
NORM_ROUNDING_MARGIN = 1.05

VMEM_LIMIT_BYTES = 56 * 1024 * 1024
NORM_GROUP_TILE = 256

BF16 = jnp.bfloat16
F32 = jnp.float32


def _dot(a, b):
    return jnp.dot(a, b, preferred_element_type=F32)


def _resident(shape):
    return pl.BlockSpec(shape, lambda *_: (0,) * len(shape), pipeline_mode=pl.Buffered(1))


def _ada_kernel(c_ref, w_ref, b_ref, lq1_ref, lk1_ref, lq2_ref, lk2_ref, gq_ref, gk_ref,
                mod_ref, lam_ref, bounded_ref):
    c = c_ref[...]
    c_act = c * jax.nn.sigmoid(c)
    mod_ref[...] = _dot(c_act.astype(BF16), w_ref[...].astype(BF16)) + b_ref[...]
    s1 = jnp.sum(lq1_ref[...] * lk1_ref[...], axis=-1, keepdims=True)
    s2 = jnp.sum(lq2_ref[...] * lk2_ref[...], axis=-1, keepdims=True)
    lam_ref[...] = jnp.exp(s1) - jnp.exp(s2) + LAMBDA_INIT
    logit_bound = (HEAD_DIM * QK_SCALE * NORM_ROUNDING_MARGIN
                   * jnp.max(jnp.abs(gq_ref[...]), axis=-1, keepdims=True)
                   * jnp.max(jnp.abs(gk_ref[...]), axis=-1, keepdims=True))
    bounded_ref[...] = jnp.where(logit_bound <= MAX_UNSHIFTED_LOGIT, 1.0, 0.0)


def _ada_call(c, w_ada, b_ada, lq1, lk1, lq2, lk2, gq, gk):
    B, D = c.shape
    n_mod = w_ada.shape[1] // D
    vec = pl.BlockSpec((1, HEAD_DIM), lambda j: (0, 0))
    scalar = pl.BlockSpec((1, 1), lambda j: (0, 0))
    return pl.pallas_call(
        _ada_kernel,
        out_shape=(jax.ShapeDtypeStruct((B, n_mod * D), F32), jax.ShapeDtypeStruct((1, 1), F32),
                   jax.ShapeDtypeStruct((1, 1), F32)),
        grid=(n_mod,),
        in_specs=[pl.BlockSpec((B, D), lambda j: (0, 0)),
                  pl.BlockSpec((D, D), lambda j: (0, j)),
                  pl.BlockSpec((1, D), lambda j: (0, j)),
                  vec, vec, vec, vec, vec, vec],
        out_specs=(pl.BlockSpec((B, D), lambda j: (0, j)), scalar, scalar),
        compiler_params=pltpu.CompilerParams(dimension_semantics=("arbitrary",)),
        name="ada",
    )(c, w_ada, b_ada, lq1, lk1, lq2, lk2, gq, gk)


def _inproj_kernel(x_ref, mod_ref, g1_ref, w_ref, gq_ref, gk_ref, gmean_ref, cw_ref,
                   q_ref, k_ref, v_ref, u_ref, sa_ref, sb_ref, cx_ref, carry_ref, *, tiles_per_seq):
    i = pl.program_id(0)
    tm, D = x_ref.shape

    @pl.when(i % tiles_per_seq == 0)
    def _():
        carry_ref[...] = jnp.zeros_like(carry_ref)

    x = x_ref[...]
    inv = lax.rsqrt(jnp.mean(x * x, axis=-1, keepdims=True) + EPS)
    h = (x * inv * g1_ref[...]) * (1.0 + mod_ref[0, 1:2, :]) + mod_ref[0, 0:1, :]
    hb = h.astype(BF16)

    yq = _dot(hb, w_ref[:, 0:D])
    yk = _dot(hb, w_ref[:, D:2 * D])
    cols = range(0, D, NORM_GROUP_TILE)
    chunks = [yq[:, c0:c0 + NORM_GROUP_TILE] for c0 in cols] + [yk[:, c0:c0 + NORM_GROUP_TILE] for c0 in cols]
    sq = jnp.concatenate([(yc * yc).astype(BF16) for yc in chunks], axis=0)
    ms = _dot(sq, gmean_ref[...])
    gq = gq_ref[...] * (HEAD_DIM ** -0.5 * LOG2_E)
    for n, yc in enumerate(chunks):
        out_ref, g = (q_ref, gq) if n < len(cols) else (k_ref, gk_ref[...])
        c0 = cols[n % len(cols)]
        out_ref[:, c0:c0 + NORM_GROUP_TILE] = (
            yc * lax.rsqrt(ms[n * tm:(n + 1) * tm, :] + EPS) * g).astype(BF16)

    prev = carry_ref[...]
    cx_ref[...] = _dot(hb, w_ref[:, 4 * D:5 * D]) * _dot(hb, w_ref[:, 5 * D:6 * D])

    v_ref[...] = _dot(hb, w_ref[:, 2 * D:3 * D]).astype(BF16)
    sa_ref[...] = jax.nn.sigmoid(_dot(hb, w_ref[:, 6 * D:7 * D])).astype(BF16)
    sb_ref[...] = jax.nn.sigmoid(_dot(hb, w_ref[:, 7 * D:8 * D])).astype(BF16)

    p = cx_ref[...]
    carry_ref[...] = p[tm - 8:, :]
    row = lax.broadcasted_iota(jnp.int32, (tm, D), 0)
    p1 = jnp.where(row == 0, prev[7:8, :], pltpu.roll(p, 1, axis=0))
    p2 = jnp.where(row == 0, prev[6:7, :], jnp.where(row == 1, prev[7:8, :], pltpu.roll(p, 2, axis=0)))
    conv = cw_ref[0:1, :] * p2 + cw_ref[1:2, :] * p1 + cw_ref[2:3, :] * p
    u_ref[...] = (_dot(hb, w_ref[:, 3 * D:4 * D]) * conv).astype(BF16)


def _inproj_call(x2, mod8, g1, w_in, gq, gk, gmean, conv_w, *, seq, tm):
    T, D = x2.shape
    tiles_per_seq = seq // tm
    row_tile = pl.BlockSpec((tm, D), lambda i: (i, 0))
    out = jax.ShapeDtypeStruct((T, D), BF16)
    return pl.pallas_call(
        functools.partial(_inproj_kernel, tiles_per_seq=tiles_per_seq),
        out_shape=(out,) * 6,
        grid=(T // tm,),
        in_specs=[row_tile,
                  pl.BlockSpec((1, 8, D), lambda i: (i // tiles_per_seq, 0, 0)),
                  _resident((1, D)),
                  _resident(w_in.shape),
                  _resident((1, NORM_GROUP_TILE)),
                  _resident((1, NORM_GROUP_TILE)),
                  _resident((NORM_GROUP_TILE, NORM_GROUP_TILE)),
                  _resident(conv_w.shape)],
        out_specs=(row_tile,) * 6,
        scratch_shapes=[pltpu.VMEM((tm, D), F32),
                        pltpu.VMEM((8, D), F32)],
        compiler_params=pltpu.CompilerParams(dimension_semantics=("arbitrary",),
                                             vmem_limit_bytes=VMEM_LIMIT_BYTES),
        name="inproj",
    )(x2, mod8, g1, w_in, gq, gk, gmean, conv_w)


def _attn_kernel(lam_ref, bounded_ref, q_ref, k_ref, v_ref, g_ref, o_ref, acc_ref, *, tq, heads):
    qi = pl.program_id(2)
    d2 = 2 * HEAD_DIM
    lane = lax.broadcasted_iota(jnp.int32, (tq, d2), 1)
    qqs = []
    for hh in range(heads):
        q = q_ref[0, :, hh * d2:(hh + 1) * d2]
        zero = jnp.zeros_like(q)
        qqs.append(jnp.concatenate([jnp.where(lane < HEAD_DIM, q, zero),
                                    jnp.where(lane >= HEAD_DIM, q, zero)], axis=0))

    def head_update(hh, start, s, stats, diagonal, shifted):
        if diagonal:
            kpos = lax.broadcasted_iota(jnp.int32, s.shape, 0)
            qpos = lax.broadcasted_iota(jnp.int32, s.shape, 1) % tq
            s = jnp.where(qpos >= kpos, s, -jnp.inf)
        if shifted:
            m_new = jnp.max(s, axis=0, keepdims=True)
            if not diagonal:
                m_new = jnp.maximum(stats[0], m_new)
                alpha = jnp.exp2(stats[0] - m_new)
            s = s - m_new
        p = jnp.exp2(s)
        p_sum = jnp.sum(p, axis=0, keepdims=True)
        v = v_ref[0, pl.ds(start, tq), hh * d2:(hh + 1) * d2]
        pv = lax.dot_general(v, p.astype(BF16), (((0,), (0,)), ((), ())), preferred_element_type=F32)
        if diagonal:
            acc_ref[hh] = pv
            return (m_new, p_sum) if shifted else (p_sum,)
        if shifted:
            acc_ref[hh] = alpha * acc_ref[hh] + pv
            return m_new, alpha * stats[1] + p_sum
        acc_ref[hh] += pv
        return (stats[0] + p_sum,)

    def step(j, stats, diagonal, shifted):
        start = pl.multiple_of(j * tq, tq)
        ss = [lax.dot_general(k_ref[0, pl.ds(start, tq), hh * d2:(hh + 1) * d2], qqs[hh],
                              (((1,), (1,)), ((), ())), preferred_element_type=F32)
              for hh in range(heads)]
        return tuple(head_update(hh, start, ss[hh], stats[hh], diagonal, shifted) for hh in range(heads))

    def run(shifted):
        zeros = jnp.zeros((1, 2 * tq), F32)
        init = ((zeros, zeros) if shifted else (zeros,),) * heads
        stats = lax.fori_loop(qi, qi + 1, lambda j, st: step(j, st, True, shifted), init)
        stats = lax.fori_loop(0, qi, lambda j, st: step(j, st, False, shifted), stats)

        lam = lam_ref[0, 0]
        for hh in range(heads):
            o = acc_ref[hh] * (1.0 / stats[hh][-1])
            od = o[:, :tq] - lam * o[:, tq:]
            inv = lax.rsqrt(jnp.mean(od * od, axis=0, keepdims=True) + EPS)
            y = od * inv * (g_ref[...] * (1.0 - LAMBDA_INIT))
            o_ref[0, :, hh * d2:(hh + 1) * d2] = y.T.astype(BF16)

    bounded = bounded_ref[0, 0] > 0.5
    pl.when(bounded)(lambda: run(False))
    pl.when(jnp.logical_not(bounded))(lambda: run(True))


def _attn_call(lam, bounded, q, k, v, subln_col, *, tq, heads):
    B, S, _ = q.shape
    d2 = 2 * HEAD_DIM * heads
    seq_block = pl.BlockSpec((1, S, d2), lambda b, h, i: (b, 0, h))
    q_block = pl.BlockSpec((1, tq, d2), lambda b, h, i: (b, i, h))
    return pl.pallas_call(
        functools.partial(_attn_kernel, tq=tq, heads=heads),
        out_shape=jax.ShapeDtypeStruct(q.shape, BF16),
        grid=(B, N_HEADS // heads, S // tq),
        in_specs=[pl.BlockSpec(memory_space=pltpu.SMEM), pl.BlockSpec(memory_space=pltpu.SMEM),
                  q_block, seq_block, seq_block,
                  pl.BlockSpec((2 * HEAD_DIM, 1), lambda b, h, i: (0, 0))],
        out_specs=q_block,
        scratch_shapes=[pltpu.VMEM((heads, 2 * HEAD_DIM, 2 * tq), F32)],
        compiler_params=pltpu.CompilerParams(
            dimension_semantics=("arbitrary", "arbitrary", "arbitrary"),
            vmem_limit_bytes=VMEM_LIMIT_BYTES),
        name="attn",
    )(lam, bounded, q, k, v, subln_col)


def _outffn_kernel(x_ref, u_ref, o_ref, sa_ref, sb_ref, mod_ref, g2_ref,
                   wa_ref, wb_ref, wo_ref, wgu_ref, wd_ref, out_ref, *, ff_chunks):
    d_ff = wd_ref.shape[0]
    ya = _dot(u_ref[...], wa_ref[...])
    yb = _dot(o_ref[...], wb_ref[...])
    m = sa_ref[...].astype(F32) * ya + sb_ref[...].astype(F32) * yb
    x1 = x_ref[...] + mod_ref[0, 2:3, :] * _dot(m.astype(BF16), wo_ref[...])

    inv = lax.rsqrt(jnp.mean(x1 * x1, axis=-1, keepdims=True) + EPS)
    h2 = ((x1 * inv * g2_ref[...]) * (1.0 + mod_ref[0, 4:5, :]) + mod_ref[0, 3:4, :]).astype(BF16)

    acc = None
    for c0, c1 in ff_chunks:
        fg = _dot(h2, wgu_ref[:, c0:c1])
        fu = _dot(h2, wgu_ref[:, d_ff + c0:d_ff + c1])
        a = (fg * jax.nn.sigmoid(fg) * fu).astype(BF16)
        part = _dot(a, wd_ref[c0:c1, :])
        acc = part if acc is None else acc + part
    out_ref[...] = x1 + mod_ref[0, 5:6, :] * acc


def _outffn_call(x2, u, o, sa, sb, mod8, g2, wa, wb, wo, wgu, wd, *, seq, tm, ff_chunks):
    T, D = x2.shape
    tiles_per_seq = seq // tm
    row_tile = pl.BlockSpec((tm, D), lambda i: (i, 0))
    return pl.pallas_call(
        functools.partial(_outffn_kernel, ff_chunks=ff_chunks),
        out_shape=jax.ShapeDtypeStruct((T, D), F32),
        grid=(T // tm,),
        in_specs=[row_tile, row_tile, row_tile, row_tile, row_tile,
                  pl.BlockSpec((1, 8, D), lambda i: (i // tiles_per_seq, 0, 0)),
                  _resident((1, D)),
                  _resident(wa.shape), _resident(wb.shape), _resident(wo.shape),
                  _resident(wgu.shape), _resident(wd.shape)],
        out_specs=row_tile,
        compiler_params=pltpu.CompilerParams(dimension_semantics=("arbitrary",),
                                             vmem_limit_bytes=VMEM_LIMIT_BYTES),
        name="outffn",
    )(x2, u, o, sa, sb, mod8, g2, wa, wb, wo, wgu, wd)


def _ff_chunks(d_ff, max_chunk):
    n = -(-d_ff // max_chunk)
    tiles = d_ff // NORM_GROUP_TILE
    assert tiles * NORM_GROUP_TILE == d_ff
    bounds = [NORM_GROUP_TILE * (tiles * c // n) for c in range(n + 1)]
    return tuple(zip(bounds[:-1], bounds[1:]))


def kernel(x, c, w_ada, b_ada, norm1_g, w_in, conv_w, q_norm_g, k_norm_g, lambda_q1, lambda_k1,
           lambda_q2, lambda_k2, subln_g, w_a_out, w_b_out, w_o, norm2_g, w_gu, w_down):
    B, S, D = x.shape
    assert w_ada.shape[0] == 1, "single-layer block"
    d_ff = w_down.shape[1]
    tm = 512
    tq = 256

    mod, lam, bounded = _ada_call(c, w_ada[0], b_ada[0][None, :], lambda_q1, lambda_k1, lambda_q2, lambda_k2,
                                  q_norm_g, k_norm_g)
    mod8 = jnp.pad(mod.reshape(B, 6, D), ((0, 0), (0, 2), (0, 0)))

    reps = NORM_GROUP_TILE // HEAD_DIM
    gq = jnp.tile(q_norm_g[0], reps)[None, :]
    gk = jnp.tile(k_norm_g[0], reps)[None, :]
    grp = jnp.arange(NORM_GROUP_TILE) // HEAD_DIM
    gmean = jnp.where(grp[:, None] == grp[None, :], 1.0 / HEAD_DIM, 0.0).astype(BF16)

    x2 = x.reshape(B * S, D)
    q, k, v, u, sa, sb = _inproj_call(x2, mod8, norm1_g, w_in[0].astype(BF16), gq, gk, gmean,
                                      conv_w[0], seq=S, tm=tm)

    o = _attn_call(lam, bounded, q.reshape(B, S, D), k.reshape(B, S, D), v.reshape(B, S, D),
                   subln_g[0][:, None], tq=tq, heads=8)

    out = _outffn_call(x2, u, o.reshape(B * S, D), sa, sb, mod8, norm2_g,
                       w_a_out[0].astype(BF16), w_b_out[0].astype(BF16), w_o[0].astype(BF16),
                       w_gu[0].astype(BF16), w_down[0].astype(BF16),
                       seq=S, tm=tm, ff_chunks=_ff_chunks(d_ff, 1536))
    return out.reshape(B, S, D)
```

```python
import functools
import math

import jax
import jax.numpy as jnp
from jax import lax
from jax.experimental import pallas as pl
from jax.experimental.pallas import tpu as pltpu

EPS = 1e-6
N_HEADS = 8
HEAD_DIM = 64
CONV_K = 3
LAMBDA_INIT = 0.8 - 0.6 * math.exp(-0.3 * 0)
LOG2_E = math.log2(math.e)
QK_SCALE = HEAD_DIM ** -0.5 * LOG2_E
MAX_UNSHIFTED_LOGIT = 60.0
NORM_ROUNDING_MARGIN = 1.05

VMEM_LIMIT_BYTES = 56 * 1024 * 1024
MXU_TILE = 256
PAIR = 2 * HEAD_DIM
ROW_TILE = 512
QUERY_TILE = 256
FF_CHUNK = 1536

BF16 = jnp.bfloat16
F32 = jnp.float32


def _dot(a, b):
    return jnp.dot(a, b, preferred_element_type=F32)


def _resident(shape):
    return pl.BlockSpec(shape, lambda *_: (0,) * len(shape), pipeline_mode=pl.Buffered(1))


def _ada_kernel(c_ref, w_ref, b_ref, lq1_ref, lk1_ref, lq2_ref, lk2_ref, gq_ref, gk_ref,
                mod_ref, lam_ref, bounded_ref):
    c = c_ref[...]
    c_act = c * jax.nn.sigmoid(c)
    mod_ref[...] = _dot(c_act.astype(BF16), w_ref[...].astype(BF16)) + b_ref[...]
    s1 = jnp.sum(lq1_ref[...] * lk1_ref[...], axis=-1, keepdims=True)
    s2 = jnp.sum(lq2_ref[...] * lk2_ref[...], axis=-1, keepdims=True)
    lam_ref[...] = jnp.exp(s1) - jnp.exp(s2) + LAMBDA_INIT
    logit_bound = (HEAD_DIM * QK_SCALE * NORM_ROUNDING_MARGIN
                   * jnp.max(jnp.abs(gq_ref[...]), axis=-1, keepdims=True)
                   * jnp.max(jnp.abs(gk_ref[...]), axis=-1, keepdims=True))
    bounded_ref[...] = jnp.where(logit_bound <= MAX_UNSHIFTED_LOGIT, 1.0, 0.0)


def _ada_call(c, w_ada, b_ada, lq1, lk1, lq2, lk2, gq, gk):
    B, D = c.shape
    n_mod = w_ada.shape[1] // D
    vec = pl.BlockSpec((1, HEAD_DIM), lambda j: (0, 0))
    scalar = pl.BlockSpec((1, 1), lambda j: (0, 0))
    return pl.pallas_call(
        _ada_kernel,
        out_shape=(jax.ShapeDtypeStruct((B, n_mod * D), F32), jax.ShapeDtypeStruct((1, 1), F32),
                   jax.ShapeDtypeStruct((1, 1), F32)),
        grid=(n_mod,),
        in_specs=[pl.BlockSpec((B, D), lambda j: (0, 0)),
                  pl.BlockSpec((D, D), lambda j: (0, j)),
                  pl.BlockSpec((1, D), lambda j: (0, j)),
                  vec, vec, vec, vec, vec, vec],
        out_specs=(pl.BlockSpec((B, D), lambda j: (0, j)), scalar, scalar),
        compiler_params=pltpu.CompilerParams(dimension_semantics=("arbitrary",)),
        name="ada",
    )(c, w_ada, b_ada, lq1, lk1, lq2, lk2, gq, gk)


def _inproj_kernel(x_ref, mod_ref, g1_ref, w_ref, gq_ref, gk_ref, cw_ref,
                   q_ref, k_ref, v_ref, u_ref, sa_ref, sb_ref, cx_ref, carry_ref, *, tiles_per_seq):
    i = pl.program_id(0)
    tm, D = x_ref.shape

    @pl.when(i % tiles_per_seq == 0)
    def _():
        carry_ref[...] = jnp.zeros_like(carry_ref)

    x = x_ref[...]
    inv = lax.rsqrt(jnp.mean(x * x, axis=-1, keepdims=True) + EPS)
    h = (x * inv * g1_ref[...]) * (1.0 + mod_ref[0, 1:2, :]) + mod_ref[0, 0:1, :]
    hb = h.astype(BF16)

    yq = _dot(hb, w_ref[:, 0:D])
    yk = _dot(hb, w_ref[:, D:2 * D])
    lo = lax.broadcasted_iota(jnp.int32, (tm, PAIR), 1) < HEAD_DIM
    for y, g, out_ref in ((yq, gq_ref[...] * QK_SCALE, q_ref), (yk, gk_ref[...], k_ref)):
        for c0 in range(0, D, PAIR):
            yc = y[:, c0:c0 + PAIR]
            sq = yc * yc
            s_lo = jnp.sum(jnp.where(lo, sq, 0.0), axis=-1, keepdims=True)
            s_hi = jnp.sum(jnp.where(lo, 0.0, sq), axis=-1, keepdims=True)
            ms = jnp.where(lo, s_lo, s_hi) * (1.0 / HEAD_DIM)
            out_ref[:, c0:c0 + PAIR] = (yc * lax.rsqrt(ms + EPS) * g).astype(BF16)

    prev = carry_ref[...]
    cx_ref[...] = _dot(hb, w_ref[:, 4 * D:5 * D]) * _dot(hb, w_ref[:, 5 * D:6 * D])

    v_ref[...] = _dot(hb, w_ref[:, 2 * D:3 * D]).astype(BF16)
    sa_ref[...] = jax.nn.sigmoid(_dot(hb, w_ref[:, 6 * D:7 * D])).astype(BF16)
    sb_ref[...] = jax.nn.sigmoid(_dot(hb, w_ref[:, 7 * D:8 * D])).astype(BF16)

    p = cx_ref[...]
    carry_ref[...] = p[tm - 8:, :]
    row = lax.broadcasted_iota(jnp.int32, (tm, D), 0)
    p1 = jnp.where(row == 0, prev[7:8, :], pltpu.roll(p, 1, axis=0))
    p2 = jnp.where(row == 0, prev[6:7, :], jnp.where(row == 1, prev[7:8, :], pltpu.roll(p, 2, axis=0)))
    conv = cw_ref[0:1, :] * p2 + cw_ref[1:2, :] * p1 + cw_ref[2:3, :] * p
    u_ref[...] = (_dot(hb, w_ref[:, 3 * D:4 * D]) * conv).astype(BF16)


def _inproj_call(x2, mod8, g1, w_in, gq, gk, conv_w, *, seq, tm):
    T, D = x2.shape
    tiles_per_seq = seq // tm
    row_tile = pl.BlockSpec((tm, D), lambda i: (i, 0))
    out = jax.ShapeDtypeStruct((T, D), BF16)
    return pl.pallas_call(
        functools.partial(_inproj_kernel, tiles_per_seq=tiles_per_seq),
        out_shape=(out,) * 6,
        grid=(T // tm,),
        in_specs=[row_tile,
                  pl.BlockSpec((1, 8, D), lambda i: (i // tiles_per_seq, 0, 0)),
                  _resident((1, D)),
                  _resident(w_in.shape),
                  _resident((1, PAIR)),
                  _resident((1, PAIR)),
                  _resident(conv_w.shape)],
        out_specs=(row_tile,) * 6,
        scratch_shapes=[pltpu.VMEM((tm, D), F32),
                        pltpu.VMEM((8, D), F32)],
        compiler_params=pltpu.CompilerParams(dimension_semantics=("arbitrary",),
                                             vmem_limit_bytes=VMEM_LIMIT_BYTES),
        name="inproj",
    )(x2, mod8, g1, w_in, gq, gk, conv_w)


def _attn_kernel(lam_ref, bounded_ref, q_ref, k_ref, v_ref, g_ref, o_ref, acc_ref, *, tq, heads):
    n_tiles = q_ref.shape[1] // tq
    d2 = PAIR
    lane = lax.broadcasted_iota(jnp.int32, (tq, d2), 1)

    def rows(t):
        return pl.ds(pl.multiple_of(t * tq, tq), tq)

    def queries(t):
        out = []
        for hh in range(heads):
            q = q_ref[0, rows(t), hh * d2:(hh + 1) * d2]
            zero = jnp.zeros_like(q)
            out.append(jnp.concatenate([jnp.where(lane < HEAD_DIM, q, zero),
                                        jnp.where(lane >= HEAD_DIM, q, zero)], axis=0))
        return out

    def logits(qqs, start, tk):
        return [lax.dot_general(k_ref[0, pl.ds(start, tk), hh * d2:(hh + 1) * d2], qqs[hh],
                                (((1,), (1,)), ((), ())), preferred_element_type=F32) for hh in range(heads)]

    def causal(s):
        kpos = lax.broadcasted_iota(jnp.int32, s.shape, 0)
        qpos = lax.broadcasted_iota(jnp.int32, s.shape, 1) % tq
        return jnp.where(qpos >= kpos, s, -jnp.inf)

    def diagonal_exp2(s):
        h = tq // 2
        tri = lax.broadcasted_iota(jnp.int32, (h, h), 0) <= lax.broadcasted_iota(jnp.int32, (h, h), 1)
        top, bottom = [], []
        for c0 in (0, tq):
            top += [jnp.exp2(jnp.where(tri, s[:h, c0:c0 + h], -jnp.inf)), jnp.exp2(s[:h, c0 + h:c0 + tq])]
            bottom += [jnp.zeros((h, h), F32), jnp.exp2(jnp.where(tri, s[h:, c0 + h:c0 + tq], -jnp.inf))]
        return jnp.concatenate([jnp.concatenate(top, axis=1), jnp.concatenate(bottom, axis=1)], axis=0)

    def weighted_values(hh, start, tk, p):
        v = v_ref[0, pl.ds(start, tk), hh * d2:(hh + 1) * d2]
        return lax.dot_general(v, p.astype(BF16), (((0,), (0,)), ((), ())), preferred_element_type=F32)

    def finalize(t, hh, acc, l):
        r = 1.0 / l
        od = acc[:, :tq] * r[:, :tq] - acc[:, tq:] * (lam_ref[0, 0] * r[:, tq:])
        inv = lax.rsqrt(jnp.mean(od * od, axis=0, keepdims=True) + EPS)
        y = od * inv * (g_ref[...] * (1.0 - LAMBDA_INIT))
        o_ref[0, rows(t), hh * d2:(hh + 1) * d2] = y.T.astype(BF16)

    def run_unshifted():
        def fold(qqs, tile, tk, sums, diagonal=False):
            start = pl.multiple_of(tile * tk, tk)
            out = []
            for hh, s in enumerate(logits(qqs, start, tk)):
                p = diagonal_exp2(s) if diagonal else jnp.exp2(s)
                p_sum = jnp.sum(p, axis=0, keepdims=True)
                pv = weighted_values(hh, start, tk, p)
                if diagonal:
                    acc_ref[hh] = pv
                    out.append(p_sum)
                else:
                    acc_ref[hh] += pv
                    out.append(sums[hh] + p_sum)
            return tuple(out)

        def below_diagonal(t, qqs, sums):
            sums = lax.fori_loop(0, t // 2, lambda j, st: fold(qqs, j, 2 * tq, st), sums)
            return lax.fori_loop(t - t % 2, t, lambda j, st: fold(qqs, j, tq, st), sums)

        def tile(t, prev_sums):
            for hh in range(heads):
                finalize(t - 1, hh, acc_ref[hh], prev_sums[hh])
            qqs = queries(t)
            return below_diagonal(t, qqs, fold(qqs, t, tq, None, True))

        sums = lax.fori_loop(1, n_tiles, tile, fold(queries(0), 0, tq, None, True))
        for hh in range(heads):
            finalize(n_tiles - 1, hh, acc_ref[hh], sums[hh])

    def run_shifted():
        def fold(qqs, tile, stats, diagonal):
            start = pl.multiple_of(tile * tq, tq)
            out = []
            for hh, s in enumerate(logits(qqs, start, tq)):
                if diagonal:
                    s = causal(s)
                    m_new = jnp.max(s, axis=0, keepdims=True)
                else:
                    m, l = stats[hh]
                    m_new = jnp.maximum(m, jnp.max(s, axis=0, keepdims=True))
                    alpha = jnp.exp2(m - m_new)
                p = jnp.exp2(s - m_new)
                p_sum = jnp.sum(p, axis=0, keepdims=True)
                pv = weighted_values(hh, start, tq, p)
                if diagonal:
                    acc_ref[hh] = pv
                    out.append((m_new, p_sum))
                else:
                    acc_ref[hh] = alpha * acc_ref[hh] + pv
                    out.append((m_new, alpha * l + p_sum))
            return tuple(out)

        def tile(t, carry):
            qqs = queries(t)
            stats = lax.fori_loop(0, t, lambda j, st: fold(qqs, j, st, False), fold(qqs, t, None, True))
            for hh in range(heads):
                finalize(t, hh, acc_ref[hh], stats[hh][1])
            return carry

        lax.fori_loop(0, n_tiles, tile, 0)

    bounded = bounded_ref[0, 0] > 0.5
    pl.when(bounded)(run_unshifted)
    pl.when(jnp.logical_not(bounded))(run_shifted)


def _attn_call(lam, bounded, q, k, v, subln_col, *, tq):
    B, S, D = q.shape
    heads = D // PAIR
    seq_block = pl.BlockSpec((1, S, D), lambda b: (b, 0, 0))
    return pl.pallas_call(
        functools.partial(_attn_kernel, tq=tq, heads=heads),
        out_shape=jax.ShapeDtypeStruct(q.shape, BF16),
        grid=(B,),
        in_specs=[pl.BlockSpec(memory_space=pltpu.SMEM), pl.BlockSpec(memory_space=pltpu.SMEM),
                  seq_block, seq_block, seq_block,
                  pl.BlockSpec((PAIR, 1), lambda b: (0, 0))],
        out_specs=seq_block,
        scratch_shapes=[pltpu.VMEM((heads, PAIR, 2 * tq), F32)],
        compiler_params=pltpu.CompilerParams(
            dimension_semantics=("arbitrary",),
            vmem_limit_bytes=VMEM_LIMIT_BYTES),
        name="attn",
    )(lam, bounded, q, k, v, subln_col)


def _outffn_kernel(x_ref, u_ref, o_ref, sa_ref, sb_ref, mod_ref, g2_ref,
                   wa_ref, wb_ref, wo_ref, wgu_ref, wd_ref, out_ref, *, ff_chunks):
    d_ff = wd_ref.shape[0]
    ya = _dot(u_ref[...], wa_ref[...])
    yb = _dot(o_ref[...], wb_ref[...])
    m = sa_ref[...].astype(F32) * ya + sb_ref[...].astype(F32) * yb
    x1 = x_ref[...] + mod_ref[0, 2:3, :] * _dot(m.astype(BF16), wo_ref[...])

    inv = lax.rsqrt(jnp.mean(x1 * x1, axis=-1, keepdims=True) + EPS)
    h2 = ((x1 * inv * g2_ref[...]) * (1.0 + mod_ref[0, 4:5, :]) + mod_ref[0, 3:4, :]).astype(BF16)

    acc = None
    for c0, c1 in ff_chunks:
        fg = _dot(h2, wgu_ref[:, c0:c1])
        fu = _dot(h2, wgu_ref[:, d_ff + c0:d_ff + c1])
        a = (fg * jax.nn.sigmoid(fg) * fu).astype(BF16)
        part = _dot(a, wd_ref[c0:c1, :])
        acc = part if acc is None else acc + part
    out_ref[...] = x1 + mod_ref[0, 5:6, :] * acc


def _outffn_call(x2, u, o, sa, sb, mod8, g2, wa, wb, wo, wgu, wd, *, seq, tm, ff_chunks):
    T, D = x2.shape
    tiles_per_seq = seq // tm
    row_tile = pl.BlockSpec((tm, D), lambda i: (i, 0))
    return pl.pallas_call(
        functools.partial(_outffn_kernel, ff_chunks=ff_chunks),
        out_shape=jax.ShapeDtypeStruct((T, D), F32),
        grid=(T // tm,),
        in_specs=[row_tile, row_tile, row_tile, row_tile, row_tile,
                  pl.BlockSpec((1, 8, D), lambda i: (i // tiles_per_seq, 0, 0)),
                  _resident((1, D)),
                  _resident(wa.shape), _resident(wb.shape), _resident(wo.shape),
                  _resident(wgu.shape), _resident(wd.shape)],
        out_specs=row_tile,
        compiler_params=pltpu.CompilerParams(dimension_semantics=("arbitrary",),
                                             vmem_limit_bytes=VMEM_LIMIT_BYTES),
        name="outffn",
    )(x2, u, o, sa, sb, mod8, g2, wa, wb, wo, wgu, wd)


def _ff_chunks(d_ff, max_chunk):
    n = -(-d_ff // max_chunk)
    tiles = d_ff // MXU_TILE
    assert tiles * MXU_TILE == d_ff
    bounds = [MXU_TILE * (tiles * c // n) for c in range(n + 1)]
    return tuple(zip(bounds[:-1], bounds[1:]))


def kernel(x, c, w_ada, b_ada, norm1_g, w_in, conv_w, q_norm_g, k_norm_g, lambda_q1, lambda_k1,
           lambda_q2, lambda_k2, subln_g, w_a_out, w_b_out, w_o, norm2_g, w_gu, w_down):
    B, S, D = x.shape
    assert w_ada.shape[0] == 1, "single-layer block"
    assert D == N_HEADS * PAIR and S % ROW_TILE == 0 and S % QUERY_TILE == 0
    d_ff = w_down.shape[1]
    tm, tq = ROW_TILE, QUERY_TILE

    mod, lam, bounded = _ada_call(c, w_ada[0], b_ada[0][None, :], lambda_q1, lambda_k1, lambda_q2, lambda_k2,
                                  q_norm_g, k_norm_g)
    mod8 = jnp.pad(mod.reshape(B, 6, D), ((0, 0), (0, 2), (0, 0)))

    gq = jnp.tile(q_norm_g[0], PAIR // HEAD_DIM)[None, :]
    gk = jnp.tile(k_norm_g[0], PAIR // HEAD_DIM)[None, :]

    x2 = x.reshape(B * S, D)
    q, k, v, u, sa, sb = _inproj_call(x2, mod8, norm1_g, w_in[0].astype(BF16), gq, gk,
                                      conv_w[0], seq=S, tm=tm)

    o = _attn_call(lam, bounded, q.reshape(B, S, D), k.reshape(B, S, D), v.reshape(B, S, D),
                   subln_g[0][:, None], tq=tq)

    out = _outffn_call(x2, u, o.reshape(B * S, D), sa, sb, mod8, norm2_g,
                       w_a_out[0].astype(BF16), w_b_out[0].astype(BF16), w_o[0].astype(BF16),
                       w_gu[0].astype(BF16), w_down[0].astype(BF16),
                       seq=S, tm=tm, ff_chunks=_ff_chunks(d_ff, FF_CHUNK))
    return out.reshape(B, S, D)
```

```python
import functools
import math

import jax
import jax.numpy as jnp
from jax import lax
from jax.experimental import pallas as pl
from jax.experimental.pallas import tpu as pltpu

EPS = 1e-6
N_HEADS = 8
HEAD_DIM = 64
CONV_K = 3
LAMBDA_INIT = 0.8 - 0.6 * math.exp(-0.3 * 0)
LOG2_E = math.log2(math.e)
QK_SCALE = HEAD_DIM ** -0.5 * LOG2_E
MAX_UNSHIFTED_LOGIT = 40.0
NORM_ROUNDING_MARGIN = 1.05

VMEM_LIMIT_BYTES = 56 * 1024 * 1024
MXU_TILE = 256
PAIR = 2 * HEAD_DIM
ROW_TILE = 512
QUERY_TILE = 256
FF_CHUNK = 1536

BF16 = jnp.bfloat16
F32 = jnp.float32


def _dot(a, b):
    return jnp.dot(a, b, preferred_element_type=F32)


def _resident(shape):
    return pl.BlockSpec(shape, lambda *_: (0,) * len(shape), pipeline_mode=pl.Buffered(1))


def _ada_kernel(c_ref, w_ref, b_ref, lq1_ref, lk1_ref, lq2_ref, lk2_ref, gq_ref, gk_ref,
                mod_ref, lam_ref, bounded_ref):
    c = c_ref[...]
    c_act = c * jax.nn.sigmoid(c)
    mod_ref[...] = _dot(c_act.astype(BF16), w_ref[...].astype(BF16)) + b_ref[...]
    s1 = jnp.sum(lq1_ref[...] * lk1_ref[...], axis=-1, keepdims=True)
    s2 = jnp.sum(lq2_ref[...] * lk2_ref[...], axis=-1, keepdims=True)
    lam_ref[...] = jnp.exp(s1) - jnp.exp(s2) + LAMBDA_INIT
    logit_bound = (HEAD_DIM * QK_SCALE * NORM_ROUNDING_MARGIN
                   * jnp.max(jnp.abs(gq_ref[...]), axis=-1, keepdims=True)
                   * jnp.max(jnp.abs(gk_ref[...]), axis=-1, keepdims=True))
    bounded_ref[...] = jnp.where(logit_bound <= MAX_UNSHIFTED_LOGIT, 1.0, 0.0)


def _ada_call(c, w_ada, b_ada, lq1, lk1, lq2, lk2, gq, gk):
    B, D = c.shape
    n_mod = w_ada.shape[1] // D
    vec = pl.BlockSpec((1, HEAD_DIM), lambda j: (0, 0))
    scalar = pl.BlockSpec((1, 1), lambda j: (0, 0))
    return pl.pallas_call(
        _ada_kernel,
        out_shape=(jax.ShapeDtypeStruct((B, n_mod * D), F32), jax.ShapeDtypeStruct((1, 1), F32),
                   jax.ShapeDtypeStruct((1, 1), F32)),
        grid=(n_mod,),
        in_specs=[pl.BlockSpec((B, D), lambda j: (0, 0)),
                  pl.BlockSpec((D, D), lambda j: (0, j)),
                  pl.BlockSpec((1, D), lambda j: (0, j)),
                  vec, vec, vec, vec, vec, vec],
        out_specs=(pl.BlockSpec((B, D), lambda j: (0, j)), scalar, scalar),
        compiler_params=pltpu.CompilerParams(dimension_semantics=("arbitrary",)),
        name="ada",
    )(c, w_ada, b_ada, lq1, lk1, lq2, lk2, gq, gk)


def _inproj_kernel(x_ref, mod_ref, g1_ref, w_ref, gq_ref, gk_ref, cw_ref,
                   q_ref, k_ref, v_ref, u_ref, sa_ref, sb_ref, cx_ref, carry_ref, *, tiles_per_seq):
    i = pl.program_id(0)
    tm, D = x_ref.shape

    @pl.when(i % tiles_per_seq == 0)
    def _():
        carry_ref[...] = jnp.zeros_like(carry_ref)

    x = x_ref[...]
    inv = lax.rsqrt(jnp.mean(x * x, axis=-1, keepdims=True) + EPS)
    h = (x * inv * g1_ref[...]) * (1.0 + mod_ref[0, 1:2, :]) + mod_ref[0, 0:1, :]
    hb = h.astype(BF16)

    yq = _dot(hb, w_ref[:, 0:D])
    yk = _dot(hb, w_ref[:, D:2 * D])
    lo = lax.broadcasted_iota(jnp.int32, (tm, PAIR), 1) < HEAD_DIM
    for y, g, out_ref in ((yq, gq_ref[...] * QK_SCALE, q_ref), (yk, gk_ref[...], k_ref)):
        for c0 in range(0, D, PAIR):
            yc = y[:, c0:c0 + PAIR]
            sq = yc * yc
            s_lo = jnp.sum(jnp.where(lo, sq, 0.0), axis=-1, keepdims=True)
            s_hi = jnp.sum(jnp.where(lo, 0.0, sq), axis=-1, keepdims=True)
            ms = jnp.where(lo, s_lo, s_hi) * (1.0 / HEAD_DIM)
            out_ref[:, c0:c0 + PAIR] = (yc * lax.rsqrt(ms + EPS) * g).astype(BF16)

    prev = carry_ref[...]
    cx_ref[...] = _dot(hb, w_ref[:, 4 * D:5 * D]) * _dot(hb, w_ref[:, 5 * D:6 * D])

    v_ref[...] = _dot(hb, w_ref[:, 2 * D:3 * D]).astype(BF16)
    sa_ref[...] = jax.nn.sigmoid(_dot(hb, w_ref[:, 6 * D:7 * D])).astype(BF16)
    sb_ref[...] = jax.nn.sigmoid(_dot(hb, w_ref[:, 7 * D:8 * D])).astype(BF16)

    p = cx_ref[...]
    carry_ref[...] = p[tm - 8:, :]
    row = lax.broadcasted_iota(jnp.int32, (tm, D), 0)
    p1 = jnp.where(row == 0, prev[7:8, :], pltpu.roll(p, 1, axis=0))
    p2 = jnp.where(row == 0, prev[6:7, :], jnp.where(row == 1, prev[7:8, :], pltpu.roll(p, 2, axis=0)))
    conv = cw_ref[0:1, :] * p2 + cw_ref[1:2, :] * p1 + cw_ref[2:3, :] * p
    u_ref[...] = (_dot(hb, w_ref[:, 3 * D:4 * D]) * conv).astype(BF16)


def _inproj_call(x2, mod8, g1, w_in, gq, gk, conv_w, *, seq, tm):
    T, D = x2.shape
    tiles_per_seq = seq // tm
    row_tile = pl.BlockSpec((tm, D), lambda i: (i, 0))
    out = jax.ShapeDtypeStruct((T, D), BF16)
    return pl.pallas_call(
        functools.partial(_inproj_kernel, tiles_per_seq=tiles_per_seq),
        out_shape=(out,) * 6,
        grid=(T // tm,),
        in_specs=[row_tile,
                  pl.BlockSpec((1, 8, D), lambda i: (i // tiles_per_seq, 0, 0)),
                  _resident((1, D)),
                  _resident(w_in.shape),
                  _resident((1, PAIR)),
                  _resident((1, PAIR)),
                  _resident(conv_w.shape)],
        out_specs=(row_tile,) * 6,
        scratch_shapes=[pltpu.VMEM((tm, D), F32),
                        pltpu.VMEM((8, D), F32)],
        compiler_params=pltpu.CompilerParams(dimension_semantics=("arbitrary",),
                                             vmem_limit_bytes=VMEM_LIMIT_BYTES),
        name="inproj",
    )(x2, mod8, g1, w_in, gq, gk, conv_w)


def _attn_kernel(lam_ref, bounded_ref, q_ref, k_ref, v_ref, g_ref, o_ref, acc_ref, *, tq, heads):
    n_tiles = q_ref.shape[1] // tq
    d2 = PAIR
    lane = lax.broadcasted_iota(jnp.int32, (tq, d2), 1)

    def rows(t):
        return pl.ds(pl.multiple_of(t * tq, tq), tq)

    def queries(t):
        out = []
        for hh in range(heads):
            q = q_ref[0, rows(t), hh * d2:(hh + 1) * d2]
            zero = jnp.zeros_like(q)
            out.append(jnp.concatenate([jnp.where(lane < HEAD_DIM, q, zero),
                                        jnp.where(lane >= HEAD_DIM, q, zero)], axis=0))
        return out

    def logits(qqs, start, tk):
        return [lax.dot_general(k_ref[0, pl.ds(start, tk), hh * d2:(hh + 1) * d2], qqs[hh],
                                (((1,), (1,)), ((), ())), preferred_element_type=F32) for hh in range(heads)]

    def causal(s):
        kpos = lax.broadcasted_iota(jnp.int32, s.shape, 0)
        qpos = lax.broadcasted_iota(jnp.int32, s.shape, 1) % tq
        return jnp.where(qpos >= kpos, s, -jnp.inf)

    def diagonal_exp2(s):
        h = tq // 2
        tri = lax.broadcasted_iota(jnp.int32, (h, h), 0) <= lax.broadcasted_iota(jnp.int32, (h, h), 1)
        top, bottom = [], []
        for c0 in (0, tq):
            top += [jnp.exp2(jnp.where(tri, s[:h, c0:c0 + h], -jnp.inf)), jnp.exp2(s[:h, c0 + h:c0 + tq])]
            bottom += [jnp.zeros((h, h), F32), jnp.exp2(jnp.where(tri, s[h:, c0 + h:c0 + tq], -jnp.inf))]
        return jnp.concatenate([jnp.concatenate(top, axis=1), jnp.concatenate(bottom, axis=1)], axis=0)

    def weighted_values(hh, start, tk, p):
        v = v_ref[0, pl.ds(start, tk), hh * d2:(hh + 1) * d2]
        return lax.dot_general(v, p.astype(BF16), (((0,), (0,)), ((), ())), preferred_element_type=F32)

    def finalize(t, hh, acc, l):
        r = 1.0 / l
        od = acc[:, :tq] * r[:, :tq] - acc[:, tq:] * (lam_ref[0, 0] * r[:, tq:])
        inv = lax.rsqrt(jnp.mean(od * od, axis=0, keepdims=True) + EPS)
        y = od * inv * (g_ref[...] * (1.0 - LAMBDA_INIT))
        o_ref[0, rows(t), hh * d2:(hh + 1) * d2] = y.T.astype(BF16)

    def run_unshifted():
        def fold(qqs, tile, tk, sums, diagonal=False):
            start = pl.multiple_of(tile * tk, tk)
            out = []
            for hh, s in enumerate(logits(qqs, start, tk)):
                p = diagonal_exp2(s) if diagonal else jnp.exp2(s)
                p_sum = jnp.sum(p, axis=0, keepdims=True)
                pv = weighted_values(hh, start, tk, p)
                if diagonal:
                    acc_ref[hh] = pv
                    out.append(p_sum)
                else:
                    acc_ref[hh] += pv
                    out.append(sums[hh] + p_sum)
            return tuple(out)

        def below_diagonal(t, qqs, sums):
            sums = lax.fori_loop(0, t // 2, lambda j, st: fold(qqs, j, 2 * tq, st), sums)
            return lax.fori_loop(t - t % 2, t, lambda j, st: fold(qqs, j, tq, st), sums)

        def tile(t, prev_sums):
            for hh in range(heads):
                finalize(t - 1, hh, acc_ref[hh], prev_sums[hh])
            qqs = queries(t)
            return below_diagonal(t, qqs, fold(qqs, t, tq, None, True))

        sums = lax.fori_loop(1, n_tiles, tile, fold(queries(0), 0, tq, None, True))
        for hh in range(heads):
            finalize(n_tiles - 1, hh, acc_ref[hh], sums[hh])

    def run_shifted():
        def fold(qqs, tile, stats, diagonal):
            start = pl.multiple_of(tile * tq, tq)
            out = []
            for hh, s in enumerate(logits(qqs, start, tq)):
                if diagonal:
                    s = causal(s)
                    m_new = jnp.max(s, axis=0, keepdims=True)
                else:
                    m, l = stats[hh]
                    m_new = jnp.maximum(m, jnp.max(s, axis=0, keepdims=True))
                    alpha = jnp.exp2(m - m_new)
                p = jnp.exp2(s - m_new)
                p_sum = jnp.sum(p, axis=0, keepdims=True)
                pv = weighted_values(hh, start, tq, p)
                if diagonal:
                    acc_ref[hh] = pv
                    out.append((m_new, p_sum))
                else:
                    acc_ref[hh] = alpha * acc_ref[hh] + pv
                    out.append((m_new, alpha * l + p_sum))
            return tuple(out)

        def tile(t, carry):
            qqs = queries(t)
            stats = lax.fori_loop(0, t, lambda j, st: fold(qqs, j, st, False), fold(qqs, t, None, True))
            for hh in range(heads):
                finalize(t, hh, acc_ref[hh], stats[hh][1])
            return carry

        lax.fori_loop(0, n_tiles, tile, 0)

    bounded = bounded_ref[0, 0] > 0.5
    pl.when(bounded)(run_unshifted)
    pl.when(jnp.logical_not(bounded))(run_shifted)


def _attn_call(lam, bounded, q, k, v, subln_col, *, tq):
    B, S, D = q.shape
    heads = D // PAIR
    seq_block = pl.BlockSpec((1, S, D), lambda b: (b, 0, 0))
    return pl.pallas_call(
        functools.partial(_attn_kernel, tq=tq, heads=heads),
        out_shape=jax.ShapeDtypeStruct(q.shape, BF16),
        grid=(B,),
        in_specs=[pl.BlockSpec(memory_space=pltpu.SMEM), pl.BlockSpec(memory_space=pltpu.SMEM),
                  seq_block, seq_block, seq_block,
                  pl.BlockSpec((PAIR, 1), lambda b: (0, 0))],
        out_specs=seq_block,
        scratch_shapes=[pltpu.VMEM((heads, PAIR, 2 * tq), F32)],
        compiler_params=pltpu.CompilerParams(
            dimension_semantics=("arbitrary",),
            vmem_limit_bytes=VMEM_LIMIT_BYTES),
        name="attn",
    )(lam, bounded, q, k, v, subln_col)


def _outffn_kernel(x_ref, u_ref, o_ref, sa_ref, sb_ref, mod_ref, g2_ref,
                   wa_ref, wb_ref, wo_ref, wgu_ref, wd_ref, out_ref, *, ff_chunks):
    d_ff = wd_ref.shape[0]
    ya = _dot(u_ref[...], wa_ref[...])
    yb = _dot(o_ref[...], wb_ref[...])
    m = sa_ref[...].astype(F32) * ya + sb_ref[...].astype(F32) * yb
    x1 = x_ref[...] + mod_ref[0, 2:3, :] * _dot(m.astype(BF16), wo_ref[...])

    inv = lax.rsqrt(jnp.mean(x1 * x1, axis=-1, keepdims=True) + EPS)
    h2 = ((x1 * inv * g2_ref[...]) * (1.0 + mod_ref[0, 4:5, :]) + mod_ref[0, 3:4, :]).astype(BF16)

    acc = None
    for c0, c1 in ff_chunks:
        fg = _dot(h2, wgu_ref[:, c0:c1])
        fu = _dot(h2, wgu_ref[:, d_ff + c0:d_ff + c1])
        a = (fg * jax.nn.sigmoid(fg) * fu).astype(BF16)
        part = _dot(a, wd_ref[c0:c1, :])
        acc = part if acc is None else acc + part
    out_ref[...] = x1 + mod_ref[0, 5:6, :] * acc


def _outffn_call(x2, u, o, sa, sb, mod8, g2, wa, wb, wo, wgu, wd, *, seq, tm, ff_chunks):
    T, D = x2.shape
    tiles_per_seq = seq // tm
    row_tile = pl.BlockSpec((tm, D), lambda i: (i, 0))
    return pl.pallas_call(
        functools.partial(_outffn_kernel, ff_chunks=ff_chunks),
        out_shape=jax.ShapeDtypeStruct((T, D), F32),
        grid=(T // tm,),
        in_specs=[row_tile, row_tile, row_tile, row_tile, row_tile,
                  pl.BlockSpec((1, 8, D), lambda i: (i // tiles_per_seq, 0, 0)),
                  _resident((1, D)),
                  _resident(wa.shape), _resident(wb.shape), _resident(wo.shape),
                  _resident(wgu.shape), _resident(wd.shape)],
        out_specs=row_tile,
        compiler_params=pltpu.CompilerParams(dimension_semantics=("arbitrary",),
                                             vmem_limit_bytes=VMEM_LIMIT_BYTES),
        name="outffn",
    )(x2, u, o, sa, sb, mod8, g2, wa, wb, wo, wgu, wd)


def _ff_chunks(d_ff, max_chunk):
    n = -(-d_ff // max_chunk)
    tiles = d_ff // MXU_TILE
    assert tiles * MXU_TILE == d_ff
    bounds = [MXU_TILE * (tiles * c // n) for c in range(n + 1)]
    return tuple(zip(bounds[:-1], bounds[1:]))


def kernel(x, c, w_ada, b_ada, norm1_g, w_in, conv_w, q_norm_g, k_norm_g, lambda_q1, lambda_k1,
           lambda_q2, lambda_k2, subln_g, w_a_out, w_b_out, w_o, norm2_g, w_gu, w_down):
    B, S, D = x.shape
    assert w_ada.shape[0] == 1, "single-layer block"
    assert D == N_HEADS * PAIR and S % ROW_TILE == 0 and S % QUERY_TILE == 0
    d_ff = w_down.shape[1]
    tm, tq = ROW_TILE, QUERY_TILE

    mod, lam, bounded = _ada_call(c, w_ada[0], b_ada[0][None, :], lambda_q1, lambda_k1, lambda_q2, lambda_k2,
                                  q_norm_g, k_norm_g)
    mod8 = jnp.pad(mod.reshape(B, 6, D), ((0, 0), (0, 2), (0, 0)))

    gq = jnp.tile(q_norm_g[0], PAIR // HEAD_DIM)[None, :]
    gk = jnp.tile(k_norm_g[0], PAIR // HEAD_DIM)[None, :]

    x2 = x.reshape(B * S, D)
    q, k, v, u, sa, sb = _inproj_call(x2, mod8, norm1_g, w_in[0].astype(BF16), gq, gk,
                                      conv_w[0], seq=S, tm=tm)

    o = _attn_call(lam, bounded, q.reshape(B, S, D), k.reshape(B, S, D), v.reshape(B, S, D),
                   subln_g[0][:, None], tq=tq)

    out = _outffn_call(x2, u, o.reshape(B * S, D), sa, sb, mod8, norm2_g,
                       w_a_out[0].astype(BF16), w_b_out[0].astype(BF16), w_o[0].astype(BF16),
                       w_gu[0].astype(BF16), w_down[0].astype(BF16),
                       seq=S, tm=tm, ff_chunks=_ff_chunks(d_ff, FF_CHUNK))
    return out.reshape(B, S, D)
```

```python
import functools
import math

import jax
import jax.numpy as jnp
from jax import lax
from jax.experimental import pallas as pl
from jax.experimental.pallas import tpu as pltpu

EPS = 1e-6
N_HEADS = 8
HEAD_DIM = 64
LAMBDA_INIT = 0.8 - 0.6 * math.exp(-0.3 * 0)
LOG2_E = math.log2(math.e)
QK_SCALE = HEAD_DIM ** -0.5 * LOG2_E
MAX_UNSHIFTED_LOGIT = 40.0
NORM_ROUNDING_MARGIN = 1.05

VMEM_LIMIT_BYTES = 56 * 1024 * 1024
MXU_TILE = 256
PAIR = 2 * HEAD_DIM
ROW_TILE = 512
QUERY_TILE = 256
FF_CHUNK = 1536

BF16 = jnp.bfloat16
F32 = jnp.float32


def _dot(a, b):
    return jnp.dot(a, b, preferred_element_type=F32)


def _resident(shape):
    return pl.BlockSpec(shape, lambda *_: (0,) * len(shape), pipeline_mode=pl.Buffered(1))


def _ada_kernel(c_ref, w_ref, b_ref, lq1_ref, lk1_ref, lq2_ref, lk2_ref, gq_ref, gk_ref,
                mod_ref, lam_ref, bounded_ref):
    c = c_ref[...]
    c_act = c * jax.nn.sigmoid(c)
    mod_ref[...] = _dot(c_act.astype(BF16), w_ref[...].astype(BF16)) + b_ref[...]
    s1 = jnp.sum(lq1_ref[...] * lk1_ref[...], axis=-1, keepdims=True)
    s2 = jnp.sum(lq2_ref[...] * lk2_ref[...], axis=-1, keepdims=True)
    lam_ref[...] = jnp.exp(s1) - jnp.exp(s2) + LAMBDA_INIT
    logit_bound = (HEAD_DIM * QK_SCALE * NORM_ROUNDING_MARGIN
                   * jnp.max(jnp.abs(gq_ref[...]), axis=-1, keepdims=True)
                   * jnp.max(jnp.abs(gk_ref[...]), axis=-1, keepdims=True))
    bounded_ref[...] = jnp.where(logit_bound <= MAX_UNSHIFTED_LOGIT, 1.0, 0.0)


def _ada_call(c, w_ada, b_ada, lq1, lk1, lq2, lk2, gq, gk):
    B, D = c.shape
    n_mod = w_ada.shape[1] // D
    vec = pl.BlockSpec((1, HEAD_DIM), lambda j: (0, 0))
    scalar = pl.BlockSpec((1, 1), lambda j: (0, 0))
    return pl.pallas_call(
        _ada_kernel,
        out_shape=(jax.ShapeDtypeStruct((B, n_mod * D), F32), jax.ShapeDtypeStruct((1, 1), F32),
                   jax.ShapeDtypeStruct((1, 1), F32)),
        grid=(n_mod,),
        in_specs=[pl.BlockSpec((B, D), lambda j: (0, 0)),
                  pl.BlockSpec((D, D), lambda j: (0, j)),
                  pl.BlockSpec((1, D), lambda j: (0, j)),
                  vec, vec, vec, vec, vec, vec],
        out_specs=(pl.BlockSpec((B, D), lambda j: (0, j)), scalar, scalar),
        compiler_params=pltpu.CompilerParams(dimension_semantics=("arbitrary",)),
        name="ada",
    )(c, w_ada, b_ada, lq1, lk1, lq2, lk2, gq, gk)


def _inproj_kernel(x_ref, mod_ref, g1_ref, w_ref, gq_ref, gk_ref, cw_ref,
                   q_ref, k_ref, v_ref, u_ref, sa_ref, sb_ref, cx_ref, carry_ref, *, tiles_per_seq):
    i = pl.program_id(0)
    tm, D = x_ref.shape

    @pl.when(i % tiles_per_seq == 0)
    def _():
        carry_ref[...] = jnp.zeros_like(carry_ref)

    x = x_ref[...]
    inv = lax.rsqrt(jnp.mean(x * x, axis=-1, keepdims=True) + EPS)
    h = (x * inv * g1_ref[...]) * (1.0 + mod_ref[0, 1:2, :]) + mod_ref[0, 0:1, :]
    hb = h.astype(BF16)

    yq = _dot(hb, w_ref[:, 0:D])
    yk = _dot(hb, w_ref[:, D:2 * D])
    lo = lax.broadcasted_iota(jnp.int32, (tm, PAIR), 1) < HEAD_DIM
    for y, g, out_ref in ((yq, gq_ref[...] * QK_SCALE, q_ref), (yk, gk_ref[...], k_ref)):
        for c0 in range(0, D, PAIR):
            yc = y[:, c0:c0 + PAIR]
            sq = yc * yc
            s_lo = jnp.sum(jnp.where(lo, sq, 0.0), axis=-1, keepdims=True)
            s_hi = jnp.sum(jnp.where(lo, 0.0, sq), axis=-1, keepdims=True)
            ms = jnp.where(lo, s_lo, s_hi) * (1.0 / HEAD_DIM)
            out_ref[:, c0:c0 + PAIR] = (yc * lax.rsqrt(ms + EPS) * g).astype(BF16)

    prev = carry_ref[...]
    cx_ref[...] = _dot(hb, w_ref[:, 4 * D:5 * D]) * _dot(hb, w_ref[:, 5 * D:6 * D])

    v_ref[...] = _dot(hb, w_ref[:, 2 * D:3 * D]).astype(BF16)
    sa_ref[...] = jax.nn.sigmoid(_dot(hb, w_ref[:, 6 * D:7 * D])).astype(BF16)
    sb_ref[...] = jax.nn.sigmoid(_dot(hb, w_ref[:, 7 * D:8 * D])).astype(BF16)

    p = cx_ref[...]
    carry_ref[...] = p[tm - 8:, :]
    row = lax.broadcasted_iota(jnp.int32, (tm, D), 0)
    p1 = jnp.where(row == 0, prev[7:8, :], pltpu.roll(p, 1, axis=0))
    p2 = jnp.where(row == 0, prev[6:7, :], jnp.where(row == 1, prev[7:8, :], pltpu.roll(p, 2, axis=0)))
    conv = cw_ref[0:1, :] * p2 + cw_ref[1:2, :] * p1 + cw_ref[2:3, :] * p
    u_ref[...] = (_dot(hb, w_ref[:, 3 * D:4 * D]) * conv).astype(BF16)


def _inproj_call(x2, mod8, g1, w_in, gq, gk, conv_w, *, seq, tm):
    T, D = x2.shape
    tiles_per_seq = seq // tm
    row_tile = pl.BlockSpec((tm, D), lambda i: (i, 0))
    out = jax.ShapeDtypeStruct((T, D), BF16)
    return pl.pallas_call(
        functools.partial(_inproj_kernel, tiles_per_seq=tiles_per_seq),
        out_shape=(out,) * 6,
        grid=(T // tm,),
        in_specs=[row_tile,
                  pl.BlockSpec((1, 8, D), lambda i: (i // tiles_per_seq, 0, 0)),
                  _resident((1, D)),
                  _resident(w_in.shape),
                  _resident((1, PAIR)),
                  _resident((1, PAIR)),
                  _resident(conv_w.shape)],
        out_specs=(row_tile,) * 6,
        scratch_shapes=[pltpu.VMEM((tm, D), F32),
                        pltpu.VMEM((8, D), F32)],
        compiler_params=pltpu.CompilerParams(dimension_semantics=("arbitrary",),
                                             vmem_limit_bytes=VMEM_LIMIT_BYTES),
        name="inproj",
    )(x2, mod8, g1, w_in, gq, gk, conv_w)


def _attn_kernel(lam_ref, bounded_ref, q_ref, k_ref, v_ref, g_ref, o_ref, acc_ref, *, tq, heads):
    n_tiles = q_ref.shape[1] // tq
    d2 = PAIR
    lane = lax.broadcasted_iota(jnp.int32, (tq, d2), 1)

    def rows(t):
        return pl.ds(pl.multiple_of(t * tq, tq), tq)

    def queries(t):
        out = []
        for hh in range(heads):
            q = q_ref[0, rows(t), hh * d2:(hh + 1) * d2]
            zero = jnp.zeros_like(q)
            out.append(jnp.concatenate([jnp.where(lane < HEAD_DIM, q, zero),
                                        jnp.where(lane >= HEAD_DIM, q, zero)], axis=0))
        return out

    def logits(qqs, start, tk):
        return [lax.dot_general(k_ref[0, pl.ds(start, tk), hh * d2:(hh + 1) * d2], qqs[hh],
                                (((1,), (1,)), ((), ())), preferred_element_type=F32) for hh in range(heads)]

    def causal(s):
        kpos = lax.broadcasted_iota(jnp.int32, s.shape, 0)
        qpos = lax.broadcasted_iota(jnp.int32, s.shape, 1) % tq
        return jnp.where(qpos >= kpos, s, -jnp.inf)

    def diagonal_exp2(s):
        h = tq // 2
        tri = lax.broadcasted_iota(jnp.int32, (h, h), 0) <= lax.broadcasted_iota(jnp.int32, (h, h), 1)
        top, bottom = [], []
        for c0 in (0, tq):
            top += [jnp.exp2(jnp.where(tri, s[:h, c0:c0 + h], -jnp.inf)), jnp.exp2(s[:h, c0 + h:c0 + tq])]
            bottom += [jnp.zeros((h, h), F32), jnp.exp2(jnp.where(tri, s[h:, c0 + h:c0 + tq], -jnp.inf))]
        return jnp.concatenate([jnp.concatenate(top, axis=1), jnp.concatenate(bottom, axis=1)], axis=0)

    def weighted_values(hh, start, tk, p):
        v = v_ref[0, pl.ds(start, tk), hh * d2:(hh + 1) * d2]
        return lax.dot_general(v, p.astype(BF16), (((0,), (0,)), ((), ())), preferred_element_type=F32)

    def finalize(t, hh, acc, l):
        r = 1.0 / l
        od = acc[:, :tq] * r[:, :tq] - acc[:, tq:] * (lam_ref[0, 0] * r[:, tq:])
        inv = lax.rsqrt(jnp.mean(od * od, axis=0, keepdims=True) + EPS)
        y = od * inv * (g_ref[...] * (1.0 - LAMBDA_INIT))
        o_ref[0, rows(t), hh * d2:(hh + 1) * d2] = y.T.astype(BF16)

    def run_unshifted():
        def fold(qqs, start, tk, sums, diagonal=False):
            start = pl.multiple_of(start, tq)
            out = []
            for hh, s in enumerate(logits(qqs, start, tk)):
                p = diagonal_exp2(s) if diagonal else jnp.exp2(s)
                p_sum = jnp.sum(p, axis=0, keepdims=True)
                pv = weighted_values(hh, start, tk, p)
                if diagonal:
                    acc_ref[hh] = pv
                    out.append(p_sum)
                else:
                    acc_ref[hh] += pv
                    out.append(sums[hh] + p_sum)
            return tuple(out)

        def below_diagonal(t, qqs, sums):
            sums = lax.fori_loop(0, t // 4, lambda j, st: fold(qqs, j * 4 * tq, 4 * tq, st), sums)
            rest, left = (t // 4) * 4 * tq, t % 4
            for n in (1, 2, 3):
                sums = lax.fori_loop(0, jnp.int32(left == n),
                                     lambda _, st, n=n: fold(qqs, rest, n * tq, st), sums)
            return sums

        def tile(t, prev_sums):
            for hh in range(heads):
                finalize(t - 1, hh, acc_ref[hh], prev_sums[hh])
            qqs = queries(t)
            return below_diagonal(t, qqs, fold(qqs, t * tq, tq, None, True))

        sums = lax.fori_loop(1, n_tiles, tile, fold(queries(0), 0, tq, None, True))
        for hh in range(heads):
            finalize(n_tiles - 1, hh, acc_ref[hh], sums[hh])

    def run_shifted():
        def fold(qqs, tile, stats, diagonal):
            start = pl.multiple_of(tile * tq, tq)
            out = []
            for hh, s in enumerate(logits(qqs, start, tq)):
                if diagonal:
                    s = causal(s)
                    m_new = jnp.max(s, axis=0, keepdims=True)
                else:
                    m, l = stats[hh]
                    m_new = jnp.maximum(m, jnp.max(s, axis=0, keepdims=True))
                    alpha = jnp.exp2(m - m_new)
                p = jnp.exp2(s - m_new)
                p_sum = jnp.sum(p, axis=0, keepdims=True)
                pv = weighted_values(hh, start, tq, p)
                if diagonal:
                    acc_ref[hh] = pv
                    out.append((m_new, p_sum))
                else:
                    acc_ref[hh] = alpha * acc_ref[hh] + pv
                    out.append((m_new, alpha * l + p_sum))
            return tuple(out)

        def tile(t, carry):
            qqs = queries(t)
            stats = lax.fori_loop(0, t, lambda j, st: fold(qqs, j, st, False), fold(qqs, t, None, True))
            for hh in range(heads):
                finalize(t, hh, acc_ref[hh], stats[hh][1])
            return carry

        lax.fori_loop(0, n_tiles, tile, 0)

    bounded = bounded_ref[0, 0] > 0.5
    pl.when(bounded)(run_unshifted)
    pl.when(jnp.logical_not(bounded))(run_shifted)


def _attn_call(lam, bounded, q, k, v, subln_col, *, tq):
    B, S, D = q.shape
    heads = D // PAIR
    seq_block = pl.BlockSpec((1, S, D), lambda b: (b, 0, 0))
    return pl.pallas_call(
        functools.partial(_attn_kernel, tq=tq, heads=heads),
        out_shape=jax.ShapeDtypeStruct(q.shape, BF16),
        grid=(B,),
        in_specs=[pl.BlockSpec(memory_space=pltpu.SMEM), pl.BlockSpec(memory_space=pltpu.SMEM),
                  seq_block, seq_block, seq_block,
                  pl.BlockSpec((PAIR, 1), lambda b: (0, 0))],
        out_specs=seq_block,
        scratch_shapes=[pltpu.VMEM((heads, PAIR, 2 * tq), F32)],
        compiler_params=pltpu.CompilerParams(
            dimension_semantics=("arbitrary",),
            vmem_limit_bytes=VMEM_LIMIT_BYTES),
        name="attn",
    )(lam, bounded, q, k, v, subln_col)


def _outffn_kernel(x_ref, u_ref, o_ref, sa_ref, sb_ref, mod_ref, g2_ref,
                   wa_ref, wb_ref, wo_ref, wgu_ref, wd_ref, out_ref, *, ff_chunks):
    d_ff = wd_ref.shape[0]
    ya = _dot(u_ref[...], wa_ref[...])
    yb = _dot(o_ref[...], wb_ref[...])
    m = sa_ref[...].astype(F32) * ya + sb_ref[...].astype(F32) * yb
    x1 = x_ref[...] + mod_ref[0, 2:3, :] * _dot(m.astype(BF16), wo_ref[...])

    inv = lax.rsqrt(jnp.mean(x1 * x1, axis=-1, keepdims=True) + EPS)
    h2 = ((x1 * inv * g2_ref[...]) * (1.0 + mod_ref[0, 4:5, :]) + mod_ref[0, 3:4, :]).astype(BF16)

    acc = None
    for c0, c1 in ff_chunks:
        fg = _dot(h2, wgu_ref[:, c0:c1])
        fu = _dot(h2, wgu_ref[:, d_ff + c0:d_ff + c1])
        a = (fg * jax.nn.sigmoid(fg) * fu).astype(BF16)
        part = _dot(a, wd_ref[c0:c1, :])
        acc = part if acc is None else acc + part
    out_ref[...] = x1 + mod_ref[0, 5:6, :] * acc


def _outffn_call(x2, u, o, sa, sb, mod8, g2, wa, wb, wo, wgu, wd, *, seq, tm, ff_chunks):
    T, D = x2.shape
    tiles_per_seq = seq // tm
    row_tile = pl.BlockSpec((tm, D), lambda i: (i, 0))
    return pl.pallas_call(
        functools.partial(_outffn_kernel, ff_chunks=ff_chunks),
        out_shape=jax.ShapeDtypeStruct((T, D), F32),
        grid=(T // tm,),
        in_specs=[row_tile, row_tile, row_tile, row_tile, row_tile,
                  pl.BlockSpec((1, 8, D), lambda i: (i // tiles_per_seq, 0, 0)),
                  _resident((1, D)),
                  _resident(wa.shape), _resident(wb.shape), _resident(wo.shape),
                  _resident(wgu.shape), _resident(wd.shape)],
        out_specs=row_tile,
        compiler_params=pltpu.CompilerParams(dimension_semantics=("arbitrary",),
                                             vmem_limit_bytes=VMEM_LIMIT_BYTES),
        name="outffn",
    )(x2, u, o, sa, sb, mod8, g2, wa, wb, wo, wgu, wd)


def _ff_chunks(d_ff, max_chunk):
    n = -(-d_ff // max_chunk)
    tiles = d_ff // MXU_TILE
    assert tiles * MXU_TILE == d_ff
    bounds = [MXU_TILE * (tiles * c // n) for c in range(n + 1)]
    return tuple(zip(bounds[:-1], bounds[1:]))


def kernel(x, c, w_ada, b_ada, norm1_g, w_in, conv_w, q_norm_g, k_norm_g, lambda_q1, lambda_k1,
           lambda_q2, lambda_k2, subln_g, w_a_out, w_b_out, w_o, norm2_g, w_gu, w_down):
    B, S, D = x.shape
    assert w_ada.shape[0] == 1, "single-layer block"
    assert D == N_HEADS * PAIR and S % ROW_TILE == 0 and S % QUERY_TILE == 0
    d_ff = w_down.shape[1]
    tm, tq = ROW_TILE, QUERY_TILE

    mod, lam, bounded = _ada_call(c, w_ada[0], b_ada[0][None, :], lambda_q1, lambda_k1, lambda_q2, lambda_k2,
                                  q_norm_g, k_norm_g)
    mod8 = jnp.pad(mod.reshape(B, 6, D), ((0, 0), (0, 2), (0, 0)))

    gq = jnp.tile(q_norm_g[0], PAIR // HEAD_DIM)[None, :]
    gk = jnp.tile(k_norm_g[0], PAIR // HEAD_DIM)[None, :]

    x2 = x.reshape(B * S, D)
    q, k, v, u, sa, sb = _inproj_call(x2, mod8, norm1_g, w_in[0].astype(BF16), gq, gk,
                                      conv_w[0], seq=S, tm=tm)

    o = _attn_call(lam, bounded, q.reshape(B, S, D), k.reshape(B, S, D), v.reshape(B, S, D),
                   subln_g[0][:, None], tq=tq)

    out = _outffn_call(x2, u, o.reshape(B * S, D), sa, sb, mod8, norm2_g,
                       w_a_out[0].astype(BF16), w_b_out[0].astype(BF16), w_o[0].astype(BF16),
                       w_gu[0].astype(BF16), w_down[0].astype(BF16),
                       seq=S, tm=tm, ff_chunks=_ff_chunks(d_ff, FF_CHUNK))
    return out.reshape(B, S, D)
```

```python
import functools
import math

import jax
import jax.numpy as jnp
from jax import lax
from jax.experimental import pallas as pl
from jax.experimental.pallas import tpu as pltpu

EPS = 1e-6
N_HEADS = 8
HEAD_DIM = 64
LAMBDA_INIT = 0.8 - 0.6 * math.exp(-0.3 * 0)
LOG2_E = math.log2(math.e)
QK_SCALE = HEAD_DIM ** -0.5 * LOG2_E
MAX_UNSHIFTED_LOGIT = 40.0
NORM_ROUNDING_MARGIN = 1.05

VMEM_LIMIT_BYTES = 56 * 1024 * 1024
MXU_TILE = 256
PAIR = 2 * HEAD_DIM
ROW_TILE = 512
QUERY_TILE = 256
FF_CHUNK = 2816

BF16 = jnp.bfloat16
F32 = jnp.float32


def _dot(a, b):
    return jnp.dot(a, b, preferred_element_type=F32)


def _resident(shape):
    return pl.BlockSpec(shape, lambda *_: (0,) * len(shape), pipeline_mode=pl.Buffered(1))


def _ada_kernel(c_ref, w_ref, b_ref, lq1_ref, lk1_ref, lq2_ref, lk2_ref, gq_ref, gk_ref,
                mod_ref, lam_ref, bounded_ref):
    c = c_ref[...]
    c_act = c * jax.nn.sigmoid(c)
    mod_ref[...] = _dot(c_act.astype(BF16), w_ref[...].astype(BF16)) + b_ref[...]
    s1 = jnp.sum(lq1_ref[...] * lk1_ref[...], axis=-1, keepdims=True)
    s2 = jnp.sum(lq2_ref[...] * lk2_ref[...], axis=-1, keepdims=True)
    lam_ref[...] = jnp.exp(s1) - jnp.exp(s2) + LAMBDA_INIT
    logit_bound = (HEAD_DIM * QK_SCALE * NORM_ROUNDING_MARGIN
                   * jnp.max(jnp.abs(gq_ref[...]), axis=-1, keepdims=True)
                   * jnp.max(jnp.abs(gk_ref[...]), axis=-1, keepdims=True))
    bounded_ref[...] = jnp.where(logit_bound <= MAX_UNSHIFTED_LOGIT, 1.0, 0.0)


def _ada_call(c, w_ada, b_ada, lq1, lk1, lq2, lk2, gq, gk):
    B, D = c.shape
    n_mod = w_ada.shape[1] // D
    vec = pl.BlockSpec((1, HEAD_DIM), lambda j: (0, 0))
    scalar = pl.BlockSpec((1, 1), lambda j: (0, 0))
    return pl.pallas_call(
        _ada_kernel,
        out_shape=(jax.ShapeDtypeStruct((B, n_mod * D), F32), jax.ShapeDtypeStruct((1, 1), F32),
                   jax.ShapeDtypeStruct((1, 1), F32)),
        grid=(n_mod,),
        in_specs=[pl.BlockSpec((B, D), lambda j: (0, 0)),
                  pl.BlockSpec((D, D), lambda j: (0, j)),
                  pl.BlockSpec((1, D), lambda j: (0, j)),
                  vec, vec, vec, vec, vec, vec],
        out_specs=(pl.BlockSpec((B, D), lambda j: (0, j)), scalar, scalar),
        compiler_params=pltpu.CompilerParams(dimension_semantics=("arbitrary",)),
        name="ada",
    )(c, w_ada, b_ada, lq1, lk1, lq2, lk2, gq, gk)


def _inproj_kernel(x_ref, mod_ref, g1_ref, w_ref, gq_ref, gk_ref, cw_ref,
                   q_ref, k_ref, v_ref, u_ref, sa_ref, sb_ref, cx_ref, carry_ref, *, tiles_per_seq):
    i = pl.program_id(0)
    tm, D = x_ref.shape

    @pl.when(i % tiles_per_seq == 0)
    def _():
        carry_ref[...] = jnp.zeros_like(carry_ref)

    x = x_ref[...]
    inv = lax.rsqrt(jnp.mean(x * x, axis=-1, keepdims=True) + EPS)
    h = (x * inv * g1_ref[...]) * (1.0 + mod_ref[0, 1:2, :]) + mod_ref[0, 0:1, :]
    hb = h.astype(BF16)

    yq = _dot(hb, w_ref[:, 0:D])
    yk = _dot(hb, w_ref[:, D:2 * D])
    lo = lax.broadcasted_iota(jnp.int32, (tm, PAIR), 1) < HEAD_DIM
    for y, g, out_ref in ((yq, gq_ref[...] * QK_SCALE, q_ref), (yk, gk_ref[...], k_ref)):
        for c0 in range(0, D, PAIR):
            yc = y[:, c0:c0 + PAIR]
            sq = yc * yc
            s_lo = jnp.sum(jnp.where(lo, sq, 0.0), axis=-1, keepdims=True)
            s_hi = jnp.sum(jnp.where(lo, 0.0, sq), axis=-1, keepdims=True)
            ms = jnp.where(lo, s_lo, s_hi) * (1.0 / HEAD_DIM)
            out_ref[:, c0:c0 + PAIR] = (yc * lax.rsqrt(ms + EPS) * g).astype(BF16)

    prev = carry_ref[...]
    cx_ref[...] = _dot(hb, w_ref[:, 4 * D:5 * D]) * _dot(hb, w_ref[:, 5 * D:6 * D])

    v_ref[...] = _dot(hb, w_ref[:, 2 * D:3 * D]).astype(BF16)
    sa_ref[...] = jax.nn.sigmoid(_dot(hb, w_ref[:, 6 * D:7 * D])).astype(BF16)
    sb_ref[...] = jax.nn.sigmoid(_dot(hb, w_ref[:, 7 * D:8 * D])).astype(BF16)

    p = cx_ref[...]
    carry_ref[...] = p[tm - 8:, :]
    row = lax.broadcasted_iota(jnp.int32, (tm, D), 0)
    p1 = jnp.where(row == 0, prev[7:8, :], pltpu.roll(p, 1, axis=0))
    p2 = jnp.where(row == 0, prev[6:7, :], jnp.where(row == 1, prev[7:8, :], pltpu.roll(p, 2, axis=0)))
    conv = cw_ref[0:1, :] * p2 + cw_ref[1:2, :] * p1 + cw_ref[2:3, :] * p
    u_ref[...] = (_dot(hb, w_ref[:, 3 * D:4 * D]) * conv).astype(BF16)


def _inproj_call(x2, mod8, g1, w_in, gq, gk, conv_w, *, seq, tm):
    T, D = x2.shape
    tiles_per_seq = seq // tm
    row_tile = pl.BlockSpec((tm, D), lambda i: (i, 0))
    out = jax.ShapeDtypeStruct((T, D), BF16)
    return pl.pallas_call(
        functools.partial(_inproj_kernel, tiles_per_seq=tiles_per_seq),
        out_shape=(out,) * 6,
        grid=(T // tm,),
        in_specs=[row_tile,
                  pl.BlockSpec((1, 8, D), lambda i: (i // tiles_per_seq, 0, 0)),
                  _resident((1, D)),
                  _resident(w_in.shape),
                  _resident((1, PAIR)),
                  _resident((1, PAIR)),
                  _resident(conv_w.shape)],
        out_specs=(row_tile,) * 6,
        scratch_shapes=[pltpu.VMEM((tm, D), F32),
                        pltpu.VMEM((8, D), F32)],
        compiler_params=pltpu.CompilerParams(dimension_semantics=("arbitrary",),
                                             vmem_limit_bytes=VMEM_LIMIT_BYTES),
        name="inproj",
    )(x2, mod8, g1, w_in, gq, gk, conv_w)


def _attn_kernel(lam_ref, bounded_ref, q_ref, k_ref, v_ref, g_ref, o_ref, acc_ref, *, tq, heads):
    n_tiles = q_ref.shape[1] // tq
    d2 = PAIR
    lane = lax.broadcasted_iota(jnp.int32, (tq, d2), 1)

    def rows(t):
        return pl.ds(pl.multiple_of(t * tq, tq), tq)

    def queries(t):
        out = []
        for hh in range(heads):
            q = q_ref[0, rows(t), hh * d2:(hh + 1) * d2]
            zero = jnp.zeros_like(q)
            out.append(jnp.concatenate([jnp.where(lane < HEAD_DIM, q, zero),
                                        jnp.where(lane >= HEAD_DIM, q, zero)], axis=0))
        return out

    def logits(qqs, start, tk):
        return [lax.dot_general(k_ref[0, pl.ds(start, tk), hh * d2:(hh + 1) * d2], qqs[hh],
                                (((1,), (1,)), ((), ())), preferred_element_type=F32) for hh in range(heads)]

    def causal(s):
        kpos = lax.broadcasted_iota(jnp.int32, s.shape, 0)
        qpos = lax.broadcasted_iota(jnp.int32, s.shape, 1) % tq
        return jnp.where(qpos >= kpos, s, -jnp.inf)

    def diagonal_exp2(s):
        h = tq // 2
        tri = lax.broadcasted_iota(jnp.int32, (h, h), 0) <= lax.broadcasted_iota(jnp.int32, (h, h), 1)
        top, bottom = [], []
        for c0 in (0, tq):
            top += [jnp.exp2(jnp.where(tri, s[:h, c0:c0 + h], -jnp.inf)), jnp.exp2(s[:h, c0 + h:c0 + tq])]
            bottom += [jnp.zeros((h, h), F32), jnp.exp2(jnp.where(tri, s[h:, c0 + h:c0 + tq], -jnp.inf))]
        return jnp.concatenate([jnp.concatenate(top, axis=1), jnp.concatenate(bottom, axis=1)], axis=0)

    def weighted_values(hh, start, tk, p):
        v = v_ref[0, pl.ds(start, tk), hh * d2:(hh + 1) * d2]
        return lax.dot_general(v, p.astype(BF16), (((0,), (0,)), ((), ())), preferred_element_type=F32)

    def finalize(t, hh, acc, l):
        r = 1.0 / l
        od = acc[:, :tq] * r[:, :tq] - acc[:, tq:] * (lam_ref[0, 0] * r[:, tq:])
        inv = lax.rsqrt(jnp.mean(od * od, axis=0, keepdims=True) + EPS)
        y = od * inv * (g_ref[...] * (1.0 - LAMBDA_INIT))
        o_ref[0, rows(t), hh * d2:(hh + 1) * d2] = y.T.astype(BF16)

    def run_unshifted():
        def fold(qqs, start, tk, sums, diagonal=False):
            start = pl.multiple_of(start, tq)
            out = []
            for hh, s in enumerate(logits(qqs, start, tk)):
                p = diagonal_exp2(s) if diagonal else jnp.exp2(s)
                p_sum = jnp.sum(p, axis=0, keepdims=True)
                pv = weighted_values(hh, start, tk, p)
                if diagonal:
                    acc_ref[hh] = pv
                    out.append(p_sum)
                else:
                    acc_ref[hh] += pv
                    out.append(sums[hh] + p_sum)
            return tuple(out)

        def below_diagonal(t, qqs, sums):
            sums = lax.fori_loop(0, t // 4, lambda j, st: fold(qqs, j * 4 * tq, 4 * tq, st), sums)
            rest, left = (t // 4) * 4 * tq, t % 4
            for n in (1, 2, 3):
                sums = lax.fori_loop(0, jnp.int32(left == n),
                                     lambda _, st, n=n: fold(qqs, rest, n * tq, st), sums)
            return sums

        def tile(t, prev_sums):
            for hh in range(heads):
                finalize(t - 1, hh, acc_ref[hh], prev_sums[hh])
            qqs = queries(t)
            return below_diagonal(t, qqs, fold(qqs, t * tq, tq, None, True))

        sums = lax.fori_loop(1, n_tiles, tile, fold(queries(0), 0, tq, None, True))
        for hh in range(heads):
            finalize(n_tiles - 1, hh, acc_ref[hh], sums[hh])

    def run_shifted():
        def fold(qqs, tile, stats, diagonal):
            start = pl.multiple_of(tile * tq, tq)
            out = []
            for hh, s in enumerate(logits(qqs, start, tq)):
                if diagonal:
                    s = causal(s)
                    m_new = jnp.max(s, axis=0, keepdims=True)
                else:
                    m, l = stats[hh]
                    m_new = jnp.maximum(m, jnp.max(s, axis=0, keepdims=True))
                    alpha = jnp.exp2(m - m_new)
                p = jnp.exp2(s - m_new)
                p_sum = jnp.sum(p, axis=0, keepdims=True)
                pv = weighted_values(hh, start, tq, p)
                if diagonal:
                    acc_ref[hh] = pv
                    out.append((m_new, p_sum))
                else:
                    acc_ref[hh] = alpha * acc_ref[hh] + pv
                    out.append((m_new, alpha * l + p_sum))
            return tuple(out)

        def tile(t, carry):
            qqs = queries(t)
            stats = lax.fori_loop(0, t, lambda j, st: fold(qqs, j, st, False), fold(qqs, t, None, True))
            for hh in range(heads):
                finalize(t, hh, acc_ref[hh], stats[hh][1])
            return carry

        lax.fori_loop(0, n_tiles, tile, 0)

    bounded = bounded_ref[0, 0] > 0.5
    pl.when(bounded)(run_unshifted)
    pl.when(jnp.logical_not(bounded))(run_shifted)


def _attn_call(lam, bounded, q, k, v, subln_col, *, tq):
    B, S, D = q.shape
    heads = D // PAIR
    seq_block = pl.BlockSpec((1, S, D), lambda b: (b, 0, 0))
    return pl.pallas_call(
        functools.partial(_attn_kernel, tq=tq, heads=heads),
        out_shape=jax.ShapeDtypeStruct(q.shape, BF16),
        grid=(B,),
        in_specs=[pl.BlockSpec(memory_space=pltpu.SMEM), pl.BlockSpec(memory_space=pltpu.SMEM),
                  seq_block, seq_block, seq_block,
                  pl.BlockSpec((PAIR, 1), lambda b: (0, 0))],
        out_specs=seq_block,
        scratch_shapes=[pltpu.VMEM((heads, PAIR, 2 * tq), F32)],
        compiler_params=pltpu.CompilerParams(
            dimension_semantics=("arbitrary",),
            vmem_limit_bytes=VMEM_LIMIT_BYTES),
        name="attn",
    )(lam, bounded, q, k, v, subln_col)


def _outffn_kernel(x_ref, u_ref, o_ref, sa_ref, sb_ref, mod_ref, g2_ref,
                   wa_ref, wb_ref, wo_ref, wgu_ref, wd_ref, out_ref, *, ff_chunks):
    d_ff = wd_ref.shape[0]
    ya = _dot(u_ref[...], wa_ref[...])
    yb = _dot(o_ref[...], wb_ref[...])
    m = sa_ref[...].astype(F32) * ya + sb_ref[...].astype(F32) * yb
    x1 = x_ref[...] + mod_ref[0, 2:3, :] * _dot(m.astype(BF16), wo_ref[...])

    inv = lax.rsqrt(jnp.mean(x1 * x1, axis=-1, keepdims=True) + EPS)
    h2 = ((x1 * inv * g2_ref[...]) * (1.0 + mod_ref[0, 4:5, :]) + mod_ref[0, 3:4, :]).astype(BF16)

    acc = None
    for c0, c1 in ff_chunks:
        fg = _dot(h2, wgu_ref[:, c0:c1])
        fu = _dot(h2, wgu_ref[:, d_ff + c0:d_ff + c1])
        a = (fg * jax.nn.sigmoid(fg) * fu).astype(BF16)
        part = _dot(a, wd_ref[c0:c1, :])
        acc = part if acc is None else acc + part
    out_ref[...] = x1 + mod_ref[0, 5:6, :] * acc


def _outffn_call(x2, u, o, sa, sb, mod8, g2, wa, wb, wo, wgu, wd, *, seq, tm, ff_chunks):
    T, D = x2.shape
    tiles_per_seq = seq // tm
    row_tile = pl.BlockSpec((tm, D), lambda i: (i, 0))
    return pl.pallas_call(
        functools.partial(_outffn_kernel, ff_chunks=ff_chunks),
        out_shape=jax.ShapeDtypeStruct((T, D), F32),
        grid=(T // tm,),
        in_specs=[row_tile, row_tile, row_tile, row_tile, row_tile,
                  pl.BlockSpec((1, 8, D), lambda i: (i // tiles_per_seq, 0, 0)),
                  _resident((1, D)),
                  _resident(wa.shape), _resident(wb.shape), _resident(wo.shape),
                  _resident(wgu.shape), _resident(wd.shape)],
        out_specs=row_tile,
        compiler_params=pltpu.CompilerParams(dimension_semantics=("arbitrary",),
                                             vmem_limit_bytes=VMEM_LIMIT_BYTES),
        name="outffn",
    )(x2, u, o, sa, sb, mod8, g2, wa, wb, wo, wgu, wd)


def _ff_chunks(d_ff, max_chunk):
    n = -(-d_ff // max_chunk)
    tiles = d_ff // MXU_TILE
    assert tiles * MXU_TILE == d_ff
    bounds = [MXU_TILE * (tiles * c // n) for c in range(n + 1)]
    return tuple(zip(bounds[:-1], bounds[1:]))


def kernel(x, c, w_ada, b_ada, norm1_g, w_in, conv_w, q_norm_g, k_norm_g, lambda_q1, lambda_k1,
           lambda_q2, lambda_k2, subln_g, w_a_out, w_b_out, w_o, norm2_g, w_gu, w_down):
    B, S, D = x.shape
    assert w_ada.shape[0] == 1, "single-layer block"
    assert D == N_HEADS * PAIR and S % ROW_TILE == 0 and S % QUERY_TILE == 0
    d_ff = w_down.shape[1]
    tm, tq = ROW_TILE, QUERY_TILE

    mod, lam, bounded = _ada_call(c, w_ada[0], b_ada[0][None, :], lambda_q1, lambda_k1, lambda_q2, lambda_k2,
                                  q_norm_g, k_norm_g)
    mod8 = jnp.pad(mod.reshape(B, 6, D), ((0, 0), (0, 2), (0, 0)))

    gq = jnp.tile(q_norm_g[0], PAIR // HEAD_DIM)[None, :]
    gk = jnp.tile(k_norm_g[0], PAIR // HEAD_DIM)[None, :]

    x2 = x.reshape(B * S, D)
    q, k, v, u, sa, sb = _inproj_call(x2, mod8, norm1_g, w_in[0].astype(BF16), gq, gk,
                                      conv_w[0], seq=S, tm=tm)

    o = _attn_call(lam, bounded, q.reshape(B, S, D), k.reshape(B, S, D), v.reshape(B, S, D),
                   subln_g[0][:, None], tq=tq)

    out = _outffn_call(x2, u, o.reshape(B * S, D), sa, sb, mod8, norm2_g,
                       w_a_out[0].astype(BF16), w_b_out[0].astype(BF16), w_o[0].astype(BF16),
                       w_gu[0].astype(BF16), w_down[0].astype(BF16),
                       seq=S, tm=tm, ff_chunks=_ff_chunks(d_ff, FF_CHUNK))
    return out.reshape(B, S, D)
```
